```python
import math
import jax, jax.numpy as jnp
from jax import lax
import numpy as np

D_MODEL = 2048
BATCH = 1
SEQ = 8192
DEPTH = 4

N_MIXERS = 4
EPS = 1e-6
ROPE_THETA = 10000.0

CONV_WIDTH = 31
POOL_WINDOWS = (2, 4, 8, 16)
POOL_GROUPS = 4
POOL_GROUP_DIM = D_MODEL // POOL_GROUPS
DIL_PATTERNS = ((128, 1), (512, 4), (2048, 16))
N_DIL_GROUPS = 3
DIL_HEADS = 8
DIL_HEAD_DIM = 128
MLA_HEADS = 16
MLA_Q_RANK = 512
MLA_KV_RANK = 512
MLA_NOPE = 128
MLA_ROPE = 64
MLA_V = 128
MLA_QK = MLA_NOPE + MLA_ROPE
Q_BLOCK = 128
N_EXPERTS = 16
EXPERT_FF = 1024
EC_FACTOR = 2


def _n_uses(m):
    return (DEPTH - m + N_MIXERS - 1) // N_MIXERS


N_A = _n_uses(0)
N_B = _n_uses(1)
N_C = _n_uses(2)
N_D = _n_uses(3)

kernel_name = "hybrid_conv_pool_dilated_mla_ecmoe_encoder"


def rms_norm(x, g):
    xf = x.astype(jnp.float32)
    y = xf * lax.rsqrt(jnp.mean(xf * xf, axis=-1, keepdims=True) + EPS)
    return (y * g.astype(jnp.float32)).astype(x.dtype)


def rope(x, positions):
    dim = x.shape[-1]
    half = dim // 2
    inv = ROPE_THETA ** (-jnp.arange(half, dtype=jnp.float32) / half)
    ang = positions.astype(jnp.float32)[..., None] * inv
    cos = jnp.cos(ang)[:, :, None, :]
    sin = jnp.sin(ang)[:, :, None, :]
    xf = x.astype(jnp.float32)
    x1, x2 = xf[..., :half], xf[..., half:]
    return jnp.concatenate([x1 * cos - x2 * sin, x2 * cos + x1 * sin], axis=-1).astype(x.dtype)


def ada_modulation(cond, w, b):
    mod = (cond @ w + b)[:, None, :]
    shift, scale, gate = jnp.split(mod, 3, axis=-1)
    return shift, scale, gate


def conformer_conv(h, w_in, dw, ln_g, ln_b, w_out):
    a, b = jnp.split(h @ w_in, 2, axis=-1)
    u = a * jax.nn.sigmoid(b)
    u = lax.conv_general_dilated(
        u, dw[:, None, :], window_strides=(1,),
        padding=[(CONV_WIDTH // 2, CONV_WIDTH // 2)],
        dimension_numbers=("NWC", "WIO", "NWC"),
        feature_group_count=u.shape[-1])
    uf = u.astype(jnp.float32)
    mu = jnp.mean(uf, axis=-1, keepdims=True)
    var = jnp.mean(jnp.square(uf - mu), axis=-1, keepdims=True)
    un = (uf - mu) * lax.rsqrt(var + EPS) * ln_g.astype(jnp.float32) + ln_b.astype(jnp.float32)
    return jax.nn.silu(un).astype(h.dtype) @ w_out


def centred_window_mean(u, r):
    B, S, C = u.shape
    uf = u.astype(jnp.float32)
    cs = jnp.concatenate([jnp.zeros((B, 1, C), jnp.float32), jnp.cumsum(uf, axis=1)], axis=1)
    t = jnp.arange(S)
    hi = jnp.minimum(t + r + 1, S)
    lo = jnp.maximum(t - r, 0)
    return (cs[:, hi] - cs[:, lo]) / (hi - lo).astype(jnp.float32)[None, :, None]


def multiscale_pool(h, w_in, w_grp, ch_scale, w_out):
    B, S, _ = h.shape
    u = (h @ w_in).reshape(B, S, POOL_GROUPS, POOL_GROUP_DIM)
    pooled = jnp.stack(
        [centred_window_mean(u[:, :, g], w // 2) for g, w in enumerate(POOL_WINDOWS)], axis=2)
    mix = (pooled - u.astype(jnp.float32)).astype(h.dtype)
    mix = jnp.einsum("bsgc,gcd->bsgd", mix, w_grp).reshape(B, S, D_MODEL)
    return (mix * ch_scale) @ w_out


def to_phases(x, d, Lp):
    B, S = x.shape[:2]
    L = S // d
    x = jnp.moveaxis(x.reshape((B, L, d) + x.shape[2:]), 2, 1)
    return jnp.pad(x, ((0, 0), (0, 0), (0, Lp - L), (0, 0), (0, 0)))


def from_phases(t, L):
    B, d = t.shape[:2]
    t = jnp.moveaxis(t[:, :, :L], 1, 2)
    return t.reshape((B, L * d) + t.shape[3:])


def dilated_window_group(q, k, v, d, half):
    B, S, H, C = q.shape
    L = S // d
    nb = -(-L // half)
    Lp = nb * half
    qb = to_phases(q, d, Lp).reshape(B, d, nb, half, H, C)

    def band(x):
        x = jnp.pad(to_phases(x, d, Lp), ((0, 0), (0, 0), (half, half), (0, 0), (0, 0)))
        x = x.reshape(B, d, nb + 2, half, H, C)
        return jnp.concatenate([x[:, :, :-2], x[:, :, 1:-1], x[:, :, 2:]], axis=3)

    kb, vb = band(k), band(v)
    s = jnp.einsum("bpnihc,bpnjhc->bpnhij", qb, kb,
                   preferred_element_type=jnp.float32) * (C ** -0.5)
    ii = jnp.arange(half)[None, :, None]
    jj = jnp.arange(3 * half)[None, None, :]
    key_l = jnp.arange(nb)[:, None, None] * half - half + jj
    valid = (jnp.abs(ii + half - jj) <= half) & (key_l >= 0) & (key_l < L)
    s = jnp.where(valid[:, None], s, -jnp.inf)
    m = jnp.max(s, axis=-1, keepdims=True)
    p = jnp.exp(s - m)
    l = jnp.sum(p, axis=-1)
    o = jnp.einsum("bpnhij,bpnjhc->bpnihc", p, vb.astype(jnp.float32)).reshape(B, d, Lp, H, C)
    m = jnp.swapaxes(m[..., 0], 3, 4).reshape(B, d, Lp, H)
    l = jnp.swapaxes(l, 3, 4).reshape(B, d, Lp, H)
    return from_phases(o, L), from_phases(m, L), from_phases(l, L)


def dilated_attention(h, positions, w_in, q_norm, k_norm, w_out):
    B, S, _ = h.shape
    u = (h @ w_in).reshape(B, S, N_DIL_GROUPS, 3, DIL_HEADS, DIL_HEAD_DIM)
    outs, ms, ls = [], [], []
    for g, (window, dil) in enumerate(DIL_PATTERNS):
        q = rope(rms_norm(u[:, :, g, 0], q_norm[g]), positions)
        k = rope(rms_norm(u[:, :, g, 1], k_norm[g]), positions)
        o, m, l = dilated_window_group(q, k, u[:, :, g, 2], dil, window // (2 * dil))
        outs.append(o)
        ms.append(m)
        ls.append(l)
    m = jnp.stack(ms, 0)
    wgt = jnp.exp(m - jnp.max(m, axis=0, keepdims=True))
    num = jnp.einsum("gbsh,gbshc->bshc", wgt, jnp.stack(outs, 0))
    den = jnp.sum(wgt * jnp.stack(ls, 0), axis=0)
    o = (num / den[..., None]).astype(h.dtype).reshape(B, S, DIL_HEADS * DIL_HEAD_DIM)
    return o @ w_out


def mla_attention(h, positions, w_in, q_a_norm, w_q_up, kv_a_norm, w_kv_up, q_norm, k_norm, w_out):
    B, S, _ = h.shape
    u = h @ w_in
    cq = u[..., :MLA_Q_RANK]
    ckv = u[..., MLA_Q_RANK:MLA_Q_RANK + MLA_KV_RANK]
    kr = u[..., MLA_Q_RANK + MLA_KV_RANK:]
    q = (rms_norm(cq, q_a_norm) @ w_q_up).reshape(B, S, MLA_HEADS, MLA_QK)
    kv = (rms_norm(ckv, kv_a_norm) @ w_kv_up).reshape(B, S, MLA_HEADS, MLA_NOPE + MLA_V)
    k_nope, v = kv[..., :MLA_NOPE], kv[..., MLA_NOPE:]
    k = jnp.concatenate(
        [k_nope, jnp.broadcast_to(kr[:, :, None, :], (B, S, MLA_HEADS, MLA_ROPE))], axis=-1)
    q = rms_norm(q, q_norm)
    k = rms_norm(k, k_norm)
    q = jnp.concatenate([q[..., :MLA_NOPE], rope(q[..., MLA_NOPE:], positions)], axis=-1)
    k = jnp.concatenate([k[..., :MLA_NOPE], rope(k[..., MLA_NOPE:], positions)], axis=-1)
    nq = S // Q_BLOCK
    qb = jnp.moveaxis(q.reshape(B, nq, Q_BLOCK, MLA_HEADS, MLA_QK), 1, 0)
    scale = MLA_QK ** -0.5

    def attend(qblk):
        s = jnp.einsum("bqhc,bkhc->bhqk", qblk, k, preferred_element_type=jnp.float32) * scale
        p = jax.nn.softmax(s, axis=-1)
        return jnp.einsum("bhqk,bkhc->bqhc", p.astype(v.dtype), v)

    o = jnp.moveaxis(lax.map(attend, qb), 0, 1).reshape(B, S, MLA_HEADS * MLA_V)
    return o @ w_out


def expert_choice_ffn(h, w_router, w_gate, w_up, w_down):
    B, S, _ = h.shape
    cap = EC_FACTOR * S // N_EXPERTS
    logits = jnp.einsum("bsd,de->bse", h, w_router, preferred_element_type=jnp.float32)
    aff = jax.nn.softmax(logits, axis=-1)
    gates, idx = lax.top_k(jnp.swapaxes(aff, 1, 2), cap)
    bidx = jnp.arange(B)[:, None, None]
    xs = h[bidx, idx]
    a = jnp.einsum("becd,edf->becf", xs, w_gate)
    b = jnp.einsum("becd,edf->becf", xs, w_up)
    y = jnp.einsum("becf,efd->becd", jax.nn.silu(a) * b, w_down)
    y = y * gates[..., None].astype(y.dtype)
    return jnp.zeros_like(h).at[bidx, idx].add(y)


def setup_inputs(seed: int = 0) -> dict:
    key = jax.random.key(seed)
    ks = iter(jax.random.split(key, 40))

    def nrm(shape, scale):
        return jax.random.normal(next(ks), shape, jnp.float32) * scale

    def gain(shape):
        return 1.0 + nrm(shape, 0.1)

    D = D_MODEL
    dil_cols = N_DIL_GROUPS * 3 * DIL_HEADS * DIL_HEAD_DIM
    mla_cols = MLA_Q_RANK + MLA_KV_RANK + MLA_ROPE
    return {
        "x": nrm((BATCH, SEQ, D), 1.0),
        "c": nrm((BATCH, D), 1.0),
        "positions": jnp.broadcast_to(jnp.arange(SEQ, dtype=jnp.int32), (BATCH, SEQ)),
        "norm_g": gain((DEPTH, 2, D)),
        "ada_w": nrm((DEPTH, 2, D, 3 * D), 0.5 * D ** -0.5),
        "ada_b": nrm((DEPTH, 2, 3 * D), 0.02),
        "conv_w_in": nrm((N_A, D, 2 * D), D ** -0.5),
        "conv_dw": nrm((N_A, CONV_WIDTH, D), CONV_WIDTH ** -0.5),
        "conv_ln_g": gain((N_A, D)),
        "conv_ln_b": nrm((N_A, D), 0.02),
        "conv_w_out": nrm((N_A, D, D), D ** -0.5),
        "pool_w_in": nrm((N_B, D, D), D ** -0.5),
        "pool_w_grp": nrm((N_B, POOL_GROUPS, POOL_GROUP_DIM, POOL_GROUP_DIM), POOL_GROUP_DIM ** -0.5),
        "pool_scale": gain((N_B, D)),
        "pool_w_out": nrm((N_B, D, D), D ** -0.5),
        "dil_w_in": nrm((N_C, D, dil_cols), D ** -0.5),
        "dil_q_norm": gain((N_C, N_DIL_GROUPS, DIL_HEAD_DIM)),
        "dil_k_norm": gain((N_C, N_DIL_GROUPS, DIL_HEAD_DIM)),
        "dil_w_out": nrm((N_C, DIL_HEADS * DIL_HEAD_DIM, D), (DIL_HEADS * DIL_HEAD_DIM) ** -0.5),
        "mla_w_in": nrm((N_D, D, mla_cols), D ** -0.5),
        "mla_q_a_norm": gain((N_D, MLA_Q_RANK)),
        "mla_w_q_up": nrm((N_D, MLA_Q_RANK, MLA_HEADS * MLA_QK), MLA_Q_RANK ** -0.5),
        "mla_kv_a_norm": gain((N_D, MLA_KV_RANK)),
        "mla_w_kv_up": nrm((N_D, MLA_KV_RANK, MLA_HEADS * (MLA_NOPE + MLA_V)), MLA_KV_RANK ** -0.5),
        "mla_q_norm": gain((N_D, MLA_QK)),
        "mla_k_norm": gain((N_D, MLA_QK)),
        "mla_w_out": nrm((N_D, MLA_HEADS * MLA_V, D), (MLA_HEADS * MLA_V) ** -0.5),
        "moe_router": nrm((DEPTH, D, N_EXPERTS), D ** -0.5),
        "moe_w_gate": nrm((DEPTH, N_EXPERTS, D, EXPERT_FF), D ** -0.5),
        "moe_w_up": nrm((DEPTH, N_EXPERTS, D, EXPERT_FF), D ** -0.5),
        "moe_w_down": nrm((DEPTH, N_EXPERTS, EXPERT_FF, D), EXPERT_FF ** -0.5),
    }


def reference(x, c, positions, norm_g, ada_w, ada_b,
              conv_w_in, conv_dw, conv_ln_g, conv_ln_b, conv_w_out,
              pool_w_in, pool_w_grp, pool_scale, pool_w_out,
              dil_w_in, dil_q_norm, dil_k_norm, dil_w_out,
              mla_w_in, mla_q_a_norm, mla_w_q_up, mla_kv_a_norm, mla_w_kv_up,
              mla_q_norm, mla_k_norm, mla_w_out,
              moe_router, moe_w_gate, moe_w_up, moe_w_down):
    cond = jax.nn.silu(c)
    for layer in range(DEPTH):
        mixer = layer % N_MIXERS
        occ = layer // N_MIXERS
        shift, scale, gate = ada_modulation(cond, ada_w[layer, 0], ada_b[layer, 0])
        h = rms_norm(x, norm_g[layer, 0]) * (1.0 + scale) + shift
        if mixer == 0:
            y = conformer_conv(h, conv_w_in[occ], conv_dw[occ], conv_ln_g[occ], conv_ln_b[occ],
                               conv_w_out[occ])
        elif mixer == 1:
            y = multiscale_pool(h, pool_w_in[occ], pool_w_grp[occ], pool_scale[occ], pool_w_out[occ])
        elif mixer == 2:
            y = dilated_attention(h, positions, dil_w_in[occ], dil_q_norm[occ], dil_k_norm[occ],
                                  dil_w_out[occ])
        else:
            y = mla_attention(h, positions, mla_w_in[occ], mla_q_a_norm[occ], mla_w_q_up[occ],
                              mla_kv_a_norm[occ], mla_w_kv_up[occ], mla_q_norm[occ],
                              mla_k_norm[occ], mla_w_out[occ])
        x = x + (1.0 + gate) * y
        shift, scale, gate = ada_modulation(cond, ada_w[layer, 1], ada_b[layer, 1])
        h = rms_norm(x, norm_g[layer, 1]) * (1.0 + scale) + shift
        y = expert_choice_ffn(h, moe_router[layer], moe_w_gate[layer], moe_w_up[layer],
                              moe_w_down[layer])
        x = x + (1.0 + gate) * y
    return x
```

```python
import functools

import jax
import jax.numpy as jnp
from jax import lax
from jax.experimental import pallas as pl
from jax.experimental.pallas import tpu as pltpu

F32 = jnp.float32
BF16 = jnp.bfloat16

D_MODEL = 2048
SEQ = 8192
DEPTH = 4
EPS = 1e-6
ROPE_THETA = 10000.0

CONV_WIDTH = 31
CONV_HALO = 16
POOL_WINDOWS = (2, 4, 8, 16)
POOL_GROUP_DIM = D_MODEL // len(POOL_WINDOWS)
POOL_HALO = 8

DIL_PATTERNS = ((128, 1), (512, 4), (2048, 16))
DIL_HEADS = 8
DIL_HEAD_DIM = 128
DIL_COLS = DIL_HEADS * DIL_HEAD_DIM
DIL_HALF = 64

MLA_HEADS = 16
MLA_Q_RANK = 512
MLA_KV_RANK = 512
MLA_NOPE = 128
MLA_ROPE = 64
MLA_V = 128
MLA_QK = MLA_NOPE + MLA_ROPE
MLA_PAD = 256

N_EXPERTS = 16
EXPERT_FF = 1024
EXPERT_CAP = 2 * SEQ // N_EXPERTS

VMEM_LIMIT_BYTES = 56 * 1024 * 1024


def _params(n_axes):
    return pltpu.CompilerParams(
        dimension_semantics=("arbitrary",) * n_axes, vmem_limit_bytes=VMEM_LIMIT_BYTES)


def _bdot(a, b):
    return jnp.dot(a, b, preferred_element_type=F32)


def _bdot_t(a, b):
    return lax.dot_general(a, b, (((1,), (1,)), ((), ())), preferred_element_type=F32)


def _ada_kernel(c_ref, w_ref, b_ref, o_ref):
    c = c_ref[...]
    cond = c * jax.nn.sigmoid(c)
    o_ref[...] = _bdot(cond.astype(BF16), w_ref[...].astype(BF16)) + b_ref[...]


def ada_all(c, ada_w, ada_b):
    n = ada_w.shape[0] * ada_w.shape[1]
    w = ada_w.reshape(n, D_MODEL, 3 * D_MODEL)
    b = ada_b.reshape(n, 1, 3 * D_MODEL)
    c8 = jnp.broadcast_to(c.reshape(1, D_MODEL), (8, D_MODEL))
    tn = 1024
    out = pl.pallas_call(
        _ada_kernel,
        grid=(n, 3 * D_MODEL // tn),
        in_specs=[
            pl.BlockSpec((8, D_MODEL), lambda l, j: (0, 0)),
            pl.BlockSpec((None, D_MODEL, tn), lambda l, j: (l, 0, j)),
            pl.BlockSpec((None, 1, tn), lambda l, j: (l, 0, j)),
        ],
        out_specs=pl.BlockSpec((None, 8, tn), lambda l, j: (l, 0, j)),
        out_shape=jax.ShapeDtypeStruct((n, 8, 3 * D_MODEL), F32),
        compiler_params=_params(2),
        name="ada_all",
    )(c8, w, b)
    return out[:, 0:1, :]


def _norm_mod(x, g, scale, shift):
    ms = jnp.mean(x * x, axis=-1, keepdims=True)
    y = x * lax.rsqrt(ms + EPS) * g
    return y * (1.0 + scale) + shift


def _nmm_plain_kernel(x_ref, g_ref, sc_ref, sh_ref, w_ref, o_ref, hn_ref):
    @pl.when(pl.program_id(1) == 0)
    def _():
        hn_ref[...] = _norm_mod(x_ref[...], g_ref[...], sc_ref[...], sh_ref[...]).astype(BF16)

    o_ref[...] = _bdot(hn_ref[...], w_ref[...].astype(BF16)).astype(o_ref.dtype)


def _nmm_glu_kernel(x_ref, g_ref, sc_ref, sh_ref, wa_ref, wb_ref, o_ref, hn_ref):
    @pl.when(pl.program_id(1) == 0)
    def _():
        hn_ref[...] = _norm_mod(x_ref[...], g_ref[...], sc_ref[...], sh_ref[...]).astype(BF16)

    hn = hn_ref[...]
    a = _bdot(hn, wa_ref[...].astype(BF16))
    b = _bdot(hn, wb_ref[...].astype(BF16))
    o_ref[...] = (a * jax.nn.sigmoid(b)).astype(o_ref.dtype)


def _nmm_dil_kernel(x_ref, g_ref, sc_ref, sh_ref, w_ref, nw_ref, qs_ref, cos_ref, sin_ref,
                    o_ref, hn_ref):
    j = pl.program_id(1)

    @pl.when(j == 0)
    def _():
        hn_ref[...] = _norm_mod(x_ref[...], g_ref[...], sc_ref[...], sh_ref[...]).astype(BF16)

    acc = _bdot(hn_ref[...], w_ref[...].astype(BF16))
    is_v = (j % 3) == 2

    @pl.when(is_v)
    def _():
        o_ref[...] = acc.astype(o_ref.dtype)

    @pl.when(jnp.logical_not(is_v))
    def _():
        nw = nw_ref[...] * qs_ref[...]
        cos = cos_ref[...]
        sin = sin_ref[...]
        for h in range(DIL_HEADS):
            a = acc[:, h * DIL_HEAD_DIM:(h + 1) * DIL_HEAD_DIM]
            ms = jnp.mean(a * a, axis=-1, keepdims=True)
            an = a * lax.rsqrt(ms + EPS) * nw
            r = an * cos + pltpu.roll(an, DIL_HEAD_DIM // 2, 1) * sin
            o_ref[:, h * DIL_HEAD_DIM:(h + 1) * DIL_HEAD_DIM] = r.astype(o_ref.dtype)


def _nmm_common_specs(tm):
    return [
        pl.BlockSpec((tm, D_MODEL), lambda i, j: (i, 0)),
        pl.BlockSpec((1, D_MODEL), lambda i, j: (0, 0)),
        pl.BlockSpec((1, D_MODEL), lambda i, j: (0, 0)),
        pl.BlockSpec((1, D_MODEL), lambda i, j: (0, 0)),
    ]


def norm_mod_matmul(x, g, scale, shift, w, *, tm, tn, out_dtype):
    m, n = x.shape[0], w.shape[1]
    return pl.pallas_call(
        _nmm_plain_kernel,
        grid=(m // tm, n // tn),
        in_specs=_nmm_common_specs(tm) + [pl.BlockSpec((D_MODEL, tn), lambda i, j: (0, j))],
        out_specs=pl.BlockSpec((tm, tn), lambda i, j: (i, j)),
        out_shape=jax.ShapeDtypeStruct((m, n), out_dtype),
        scratch_shapes=[pltpu.VMEM((tm, D_MODEL), BF16)],
        compiler_params=_params(2),
        name="norm_mod_matmul",
    )(x, g, scale, shift, w)


def norm_mod_matmul_glu(x, g, scale, shift, w, *, tm, tn, out_dtype):
    m, n = x.shape[0], w.shape[1] // 2
    nj = n // tn
    return pl.pallas_call(
        _nmm_glu_kernel,
        grid=(m // tm, nj),
        in_specs=_nmm_common_specs(tm) + [
            pl.BlockSpec((D_MODEL, tn), lambda i, j: (0, j)),
            pl.BlockSpec((D_MODEL, tn), lambda i, j: (0, j + nj)),
        ],
        out_specs=pl.BlockSpec((tm, tn), lambda i, j: (i, j)),
        out_shape=jax.ShapeDtypeStruct((m, n), out_dtype),
        scratch_shapes=[pltpu.VMEM((tm, D_MODEL), BF16)],
        compiler_params=_params(2),
        name="norm_mod_matmul_glu",
    )(x, g, scale, shift, w, w)


def norm_mod_matmul_dil(x, g, scale, shift, w, nw, qs, cos, sin, *, tm):
    m, n = x.shape[0], w.shape[1]
    tn = DIL_COLS
    return pl.pallas_call(
        _nmm_dil_kernel,
        grid=(m // tm, n // tn),
        in_specs=_nmm_common_specs(tm) + [
            pl.BlockSpec((D_MODEL, tn), lambda i, j: (0, j)),
            pl.BlockSpec((None, 1, DIL_HEAD_DIM), lambda i, j: (j, 0, 0)),
            pl.BlockSpec((None, 1, DIL_HEAD_DIM), lambda i, j: (j, 0, 0)),
            pl.BlockSpec((tm, DIL_HEAD_DIM), lambda i, j: (i, 0)),
            pl.BlockSpec((tm, DIL_HEAD_DIM), lambda i, j: (i, 0)),
        ],
        out_specs=pl.BlockSpec((tm, tn), lambda i, j: (i, j)),
        out_shape=jax.ShapeDtypeStruct((m, n), BF16),
        scratch_shapes=[pltpu.VMEM((tm, D_MODEL), BF16)],
        compiler_params=_params(2),
        name="norm_mod_matmul_dil",
    )(x, g, scale, shift, w, nw, qs, cos, sin)


def _mmres_kernel(a_ref, w_ref, x_ref, gate_ref, o_ref):
    y = _bdot(a_ref[...], w_ref[...].astype(BF16))
    o_ref[...] = x_ref[...] + (1.0 + gate_ref[...]) * y


def matmul_residual(a, w, x, gate, *, tm, tn):
    m, k = a.shape
    n = w.shape[1]
    return pl.pallas_call(
        _mmres_kernel,
        grid=(m // tm, n // tn),
        in_specs=[
            pl.BlockSpec((tm, k), lambda i, j: (i, 0)),
            pl.BlockSpec((k, tn), lambda i, j: (0, j)),
            pl.BlockSpec((tm, tn), lambda i, j: (i, j)),
            pl.BlockSpec((1, tn), lambda i, j: (0, j)),
        ],
        out_specs=pl.BlockSpec((tm, tn), lambda i, j: (i, j)),
        out_shape=jax.ShapeDtypeStruct((m, n), F32),
        compiler_params=_params(2),
        name="matmul_residual",
    )(a, w, x, gate)


def _conv_kernel(prev_ref, cur_ref, next_ref, dw_ref, lg_ref, lb_ref, o_ref, ext_ref, cv_ref, *, tr):
    i = pl.program_id(0)
    last = pl.num_programs(0) - 1
    ext_ref[0:CONV_HALO, :] = jnp.where(i > 0, prev_ref[...], 0.0)
    ext_ref[CONV_HALO:CONV_HALO + tr, :] = cur_ref[...]
    ext_ref[CONV_HALO + tr:, :] = jnp.where(i < last, next_ref[...], 0.0)
    base = CONV_HALO - CONV_WIDTH // 2
    n_chunks = D_MODEL // 128

    def conv_chunk(c, s1):
        cs = pl.ds(pl.multiple_of(c * 128, 128), 128)
        acc = jnp.zeros((tr, 128), F32)
        for k in range(CONV_WIDTH):
            acc = acc + ext_ref[base + k:base + k + tr, cs] * dw_ref[k:k + 1, cs]
        cv_ref[:, cs] = acc
        return s1 + jnp.sum(acc, axis=1, keepdims=True)

    s1 = lax.fori_loop(0, n_chunks, conv_chunk, jnp.zeros((tr, 1), F32))
    mu = s1 * (1.0 / D_MODEL)

    def var_chunk(c, s2):
        cs = pl.ds(pl.multiple_of(c * 128, 128), 128)
        dlt = cv_ref[:, cs] - mu
        return s2 + jnp.sum(dlt * dlt, axis=1, keepdims=True)

    s2 = lax.fori_loop(0, n_chunks, var_chunk, jnp.zeros((tr, 1), F32))
    rstd = lax.rsqrt(s2 * (1.0 / D_MODEL) + EPS)

    def out_chunk(c, carry):
        cs = pl.ds(pl.multiple_of(c * 128, 128), 128)
        un = (cv_ref[:, cs] - mu) * rstd * lg_ref[:, cs] + lb_ref[:, cs]
        o_ref[:, cs] = (un * jax.nn.sigmoid(un)).astype(o_ref.dtype)
        return carry

    lax.fori_loop(0, n_chunks, out_chunk, 0)


def conv_ln_swish(u, dw, ln_g, ln_b, *, tr):
    m = u.shape[0]
    hb = tr // CONV_HALO
    n_hb = m // CONV_HALO
    return pl.pallas_call(
        functools.partial(_conv_kernel, tr=tr),
        grid=(m // tr,),
        in_specs=[
            pl.BlockSpec((CONV_HALO, D_MODEL), lambda i: (jnp.maximum(i * hb - 1, 0), 0)),
            pl.BlockSpec((tr, D_MODEL), lambda i: (i, 0)),
            pl.BlockSpec((CONV_HALO, D_MODEL), lambda i: (jnp.minimum((i + 1) * hb, n_hb - 1), 0)),
            pl.BlockSpec((CONV_WIDTH, D_MODEL), lambda i: (0, 0)),
            pl.BlockSpec((1, D_MODEL), lambda i: (0, 0)),
            pl.BlockSpec((1, D_MODEL), lambda i: (0, 0)),
        ],
        out_specs=pl.BlockSpec((tr, D_MODEL), lambda i: (i, 0)),
        out_shape=jax.ShapeDtypeStruct((m, D_MODEL), BF16),
        scratch_shapes=[
            pltpu.VMEM((tr + 2 * CONV_HALO, D_MODEL), F32),
            pltpu.VMEM((tr, D_MODEL), F32),
        ],
        compiler_params=_params(1),
        name="conv_ln_swish",
    )(u, u, u, dw, ln_g, ln_b)


def _pool_kernel(prev_ref, cur_ref, next_ref, wg_ref, cs_ref, o_ref, ext_ref, mix_ref, *, tr):
    i = pl.program_id(0)
    last = pl.num_programs(0) - 1
    ext_ref[0:POOL_HALO, :] = jnp.where(i > 0, prev_ref[...], 0.0)
    ext_ref[POOL_HALO:POOL_HALO + tr, :] = cur_ref[...]
    ext_ref[POOL_HALO + tr:, :] = jnp.where(i < last, next_ref[...], 0.0)
    t = i * tr + lax.broadcasted_iota(jnp.int32, (tr, 1), 0)
    for g, window in enumerate(POOL_WINDOWS):
        r = window // 2
        cnt = (jnp.minimum(t + r + 1, SEQ) - jnp.maximum(t - r, 0)).astype(F32)
        inv_cnt = 1.0 / cnt
        for c in range(POOL_GROUP_DIM // 128):
            lo = g * POOL_GROUP_DIM + c * 128
            acc = jnp.zeros((tr, 128), F32)
            for dlt in range(-r, r + 1):
                acc = acc + ext_ref[POOL_HALO + dlt:POOL_HALO + dlt + tr, lo:lo + 128]
            mix = acc * inv_cnt - ext_ref[POOL_HALO:POOL_HALO + tr, lo:lo + 128]
            mix_ref[:, lo:lo + 128] = mix.astype(BF16)
    for g in range(len(POOL_WINDOWS)):
        cols = slice(g * POOL_GROUP_DIM, (g + 1) * POOL_GROUP_DIM)
        y = _bdot(mix_ref[:, cols], wg_ref[g].astype(BF16))
        o_ref[:, cols] = (y * cs_ref[:, cols]).astype(o_ref.dtype)


def pool_mix(u, w_grp, ch_scale, *, tr):
    m = u.shape[0]
    hb = tr // POOL_HALO
    n_hb = m // POOL_HALO
    n_grp = len(POOL_WINDOWS)
    return pl.pallas_call(
        functools.partial(_pool_kernel, tr=tr),
        grid=(m // tr,),
        in_specs=[
            pl.BlockSpec((POOL_HALO, D_MODEL), lambda i: (jnp.maximum(i * hb - 1, 0), 0)),
            pl.BlockSpec((tr, D_MODEL), lambda i: (i, 0)),
            pl.BlockSpec((POOL_HALO, D_MODEL), lambda i: (jnp.minimum((i + 1) * hb, n_hb - 1), 0)),
            pl.BlockSpec((n_grp, POOL_GROUP_DIM, POOL_GROUP_DIM), lambda i: (0, 0, 0)),
            pl.BlockSpec((1, D_MODEL), lambda i: (0, 0)),
        ],
        out_specs=pl.BlockSpec((tr, D_MODEL), lambda i: (i, 0)),
        out_shape=jax.ShapeDtypeStruct((m, D_MODEL), BF16),
        scratch_shapes=[
            pltpu.VMEM((tr + 2 * POOL_HALO, D_MODEL), F32),
            pltpu.VMEM((tr, D_MODEL), BF16),
        ],
        compiler_params=_params(1),
        name="pool_mix",
    )(u, u, u, w_grp, ch_scale)


def _dil_attn_kernel(q_ref, kp_ref, kc_ref, kn_ref, vp_ref, vc_ref, vn_ref,
                     o_ref, m_ref, l_ref, *, tq, n_slots):
    i = pl.program_id(1)
    row = lax.broadcasted_iota(jnp.int32, (tq, 3 * tq), 0)
    col = lax.broadcasted_iota(jnp.int32, (tq, 3 * tq), 1)
    key_slot = i * tq - tq + col
    valid = (jnp.abs(col - tq - row) <= DIL_HALF) & (key_slot >= 0) & (key_slot < n_slots)
    for h in range(DIL_HEADS):
        hs = slice(h * DIL_HEAD_DIM, (h + 1) * DIL_HEAD_DIM)
        q = q_ref[:, hs]
        s = jnp.concatenate(
            [_bdot_t(q, kp_ref[:, hs]), _bdot_t(q, kc_ref[:, hs]), _bdot_t(q, kn_ref[:, hs])], axis=1)
        s = jnp.where(valid, s, -jnp.inf)
        m = jnp.max(s, axis=1, keepdims=True)
        p = jnp.exp(s - m)
        l = jnp.sum(p, axis=1, keepdims=True)
        pb = p.astype(BF16)
        o = (_bdot(pb[:, 0:tq], vp_ref[:, hs]) + _bdot(pb[:, tq:2 * tq], vc_ref[:, hs])
             + _bdot(pb[:, 2 * tq:], vn_ref[:, hs]))
        o_ref[:, hs] = o
        m_ref[:, hs] = jnp.broadcast_to(m, (tq, DIL_HEAD_DIM))
        l_ref[:, hs] = jnp.broadcast_to(l, (tq, DIL_HEAD_DIM))


def dil_group_attention(qkv, group, dil, *, tq):
    n_slots = SEQ // dil
    nb = n_slots // tq
    n_slabs = qkv.shape[1] // DIL_COLS
    view = qkv.reshape(n_slots, dil * n_slabs * DIL_COLS)

    def spec(slab, shift):
        return pl.BlockSpec(
            (tq, DIL_COLS), lambda p, i: (jnp.clip(i + shift, 0, nb - 1), p * n_slabs + slab))

    out_spec = pl.BlockSpec((tq, DIL_COLS), lambda p, i: (i, p))
    out_sds = jax.ShapeDtypeStruct((n_slots, dil * DIL_COLS), F32)
    q_slab, k_slab, v_slab = 3 * group, 3 * group + 1, 3 * group + 2
    o, m, l = pl.pallas_call(
        functools.partial(_dil_attn_kernel, tq=tq, n_slots=n_slots),
        grid=(dil, nb),
        in_specs=[spec(q_slab, 0),
                  spec(k_slab, -1), spec(k_slab, 0), spec(k_slab, 1),
                  spec(v_slab, -1), spec(v_slab, 0), spec(v_slab, 1)],
        out_specs=[out_spec, out_spec, out_spec],
        out_shape=[out_sds, out_sds, out_sds],
        compiler_params=_params(2),
        name=f"dil_attn_g{group}",
    )(view, view, view, view, view, view, view)
    return (o.reshape(SEQ, DIL_COLS), m.reshape(SEQ, DIL_COLS), l.reshape(SEQ, DIL_COLS))


def _dil_merge_kernel(o0, m0, l0, o1, m1, l1, o2, m2, l2, out_ref):
    mm = jnp.maximum(jnp.maximum(m0[...], m1[...]), m2[...])
    w0 = jnp.exp(m0[...] - mm)
    w1 = jnp.exp(m1[...] - mm)
    w2 = jnp.exp(m2[...] - mm)
    num = w0 * o0[...] + w1 * o1[...] + w2 * o2[...]
    den = w0 * l0[...] + w1 * l1[...] + w2 * l2[...]
    out_ref[...] = (num / den).astype(out_ref.dtype)


def dil_merge(parts, *, tm):
    spec = pl.BlockSpec((tm, DIL_COLS), lambda i: (i, 0))
    flat = [a for part in parts for a in part]
    return pl.pallas_call(
        _dil_merge_kernel,
        grid=(SEQ // tm,),
        in_specs=[spec] * 9,
        out_specs=spec,
        out_shape=jax.ShapeDtypeStruct((SEQ, DIL_COLS), BF16),
        compiler_params=_params(1),
        name="dil_merge",
    )(*flat)


def _mla_prep_kernel(u_ref, qan_ref, kvan_ref, wq_ref, wkv_ref, qn_ref, kn_ref,
                     cos_ref, sa_ref, sb_ref, q_out, k_out, v_out, cq_s, ckv_s):
    @pl.when(pl.program_id(1) == 0)
    def _():
        cq = u_ref[:, 0:MLA_Q_RANK]
        cq_s[...] = (cq * lax.rsqrt(jnp.mean(cq * cq, axis=-1, keepdims=True) + EPS)
                     * qan_ref[...]).astype(BF16)
        ckv = u_ref[:, MLA_Q_RANK:MLA_Q_RANK + MLA_KV_RANK]
        ckv_s[...] = (ckv * lax.rsqrt(jnp.mean(ckv * ckv, axis=-1, keepdims=True) + EPS)
                      * kvan_ref[...]).astype(BF16)

    cos = cos_ref[...]
    sa = sa_ref[...]
    sb = sb_ref[...]

    def norm_rope(nope, rp, w_ref):
        ssq = jnp.sum(nope * nope, axis=-1, keepdims=True) + jnp.sum(rp * rp, axis=-1, keepdims=True)
        r = lax.rsqrt(ssq * (1.0 / MLA_QK) + EPS)
        nope_n = nope * r * w_ref[:, 0:128]
        rp_n = rp * r * w_ref[:, 128:256]
        rp_r = rp_n * cos + pltpu.roll(rp_n, 96, 1) * sa + pltpu.roll(rp_n, 32, 1) * sb
        return nope_n, rp_r

    q = _bdot(cq_s[...], wq_ref[...].astype(BF16))
    kv = _bdot(ckv_s[...], wkv_ref[...].astype(BF16))
    kr = u_ref[:, MLA_Q_RANK + MLA_KV_RANK:MLA_Q_RANK + MLA_KV_RANK + 128]
    scale = MLA_QK ** -0.5
    qn, qr = norm_rope(q[:, 0:128], q[:, 128:256], qn_ref)
    q_out[:, 0:128] = (qn * scale).astype(BF16)
    q_out[:, 128:256] = (qr * scale).astype(BF16)
    kn, kr_r = norm_rope(kv[:, 0:128], kr, kn_ref)
    k_out[:, 0:128] = kn.astype(BF16)
    k_out[:, 128:256] = kr_r.astype(BF16)
    v_out[...] = kv[:, 128:256].astype(BF16)


def mla_prep(u, q_a_norm, kv_a_norm, wq_pad, w_kv_up, qn_pad, kn_pad, cos, sa, sb, *, tm):
    m = u.shape[0]
    ucols = u.shape[1]
    vec = lambda n: pl.BlockSpec((1, n), lambda i, h: (0, 0))
    tab = pl.BlockSpec((tm, 128), lambda i, h: (i, 0))
    head_out = lambda w: pl.BlockSpec((None, tm, w), lambda i, h: (h, i, 0))
    return pl.pallas_call(
        _mla_prep_kernel,
        grid=(m // tm, MLA_HEADS),
        in_specs=[
            pl.BlockSpec((tm, ucols), lambda i, h: (i, 0)),
            vec(MLA_Q_RANK), vec(MLA_KV_RANK),
            pl.BlockSpec((MLA_Q_RANK, MLA_PAD), lambda i, h: (0, h)),
            pl.BlockSpec((MLA_KV_RANK, MLA_NOPE + MLA_V), lambda i, h: (0, h)),
            vec(MLA_PAD), vec(MLA_PAD),
            tab, tab, tab,
        ],
        out_specs=[head_out(MLA_PAD), head_out(MLA_PAD), head_out(MLA_V)],
        out_shape=[
            jax.ShapeDtypeStruct((MLA_HEADS, m, MLA_PAD), BF16),
            jax.ShapeDtypeStruct((MLA_HEADS, m, MLA_PAD), BF16),
            jax.ShapeDtypeStruct((MLA_HEADS, m, MLA_V), BF16),
        ],
        scratch_shapes=[pltpu.VMEM((tm, MLA_Q_RANK), BF16), pltpu.VMEM((tm, MLA_KV_RANK), BF16)],
        compiler_params=_params(2),
        name="mla_prep",
    )(u, q_a_norm, kv_a_norm, wq_pad, w_kv_up, qn_pad, kn_pad, cos, sa, sb)


def _mla_attn_kernel(q_ref, k_ref, v_ref, o_ref, m_s, l_s, acc_s, *, tk):
    q = q_ref[...]
    m_s[...] = jnp.full(m_s.shape, -jnp.inf, F32)
    l_s[...] = jnp.zeros(l_s.shape, F32)
    acc_s[...] = jnp.zeros(acc_s.shape, F32)

    def body(c, carry):
        ks = pl.ds(pl.multiple_of(c * tk, tk), tk)
        s = _bdot_t(q, k_ref[ks, :])
        m_prev = m_s[...]
        m_new = jnp.maximum(m_prev, jnp.max(s, axis=1, keepdims=True))
        alpha = jnp.exp(m_prev - m_new)
        p = jnp.exp(s - m_new)
        l_s[...] = alpha * l_s[...] + jnp.sum(p, axis=1, keepdims=True)
        acc_s[...] = alpha * acc_s[...] + _bdot(p.astype(BF16), v_ref[ks, :])
        m_s[...] = m_new
        return carry

    lax.fori_loop(0, k_ref.shape[0] // tk, body, 0)
    o_ref[...] = (acc_s[...] / l_s[...]).astype(o_ref.dtype)


def mla_attention(q, k, v, *, tq, tk):
    n_heads, m, _ = q.shape
    return pl.pallas_call(
        functools.partial(_mla_attn_kernel, tk=tk),
        grid=(n_heads, m // tq),
        in_specs=[
            pl.BlockSpec((None, tq, MLA_PAD), lambda h, i: (h, i, 0)),
            pl.BlockSpec((None, m, MLA_PAD), lambda h, i: (h, 0, 0)),
            pl.BlockSpec((None, m, MLA_V), lambda h, i: (h, 0, 0)),
        ],
        out_specs=pl.BlockSpec((tq, MLA_V), lambda h, i: (i, h)),
        out_shape=jax.ShapeDtypeStruct((m, n_heads * MLA_V), BF16),
        scratch_shapes=[
            pltpu.VMEM((tq, 1), F32), pltpu.VMEM((tq, 1), F32), pltpu.VMEM((tq, MLA_V), F32)],
        compiler_params=_params(2),
        name="mla_attention",
    )(q, k, v)


def _router_kernel(x_ref, g_ref, sc_ref, sh_ref, wr_ref, hn_out, aff_out):
    hn = _norm_mod(x_ref[...], g_ref[...], sc_ref[...], sh_ref[...])
    hn_out[...] = hn.astype(BF16)
    logits = jnp.dot(hn, wr_ref[...], preferred_element_type=F32, precision=lax.Precision.HIGHEST)
    z = jnp.exp(logits - jnp.max(logits, axis=-1, keepdims=True))
    aff_out[...] = z / jnp.sum(z, axis=-1, keepdims=True)


def moe_router(x, g, scale, shift, w_router, *, tm):
    m = x.shape[0]
    vec = pl.BlockSpec((1, D_MODEL), lambda i: (0, 0))
    return pl.pallas_call(
        _router_kernel,
        grid=(m // tm,),
        in_specs=[pl.BlockSpec((tm, D_MODEL), lambda i: (i, 0)), vec, vec, vec,
                  pl.BlockSpec((D_MODEL, N_EXPERTS), lambda i: (0, 0))],
        out_specs=[pl.BlockSpec((tm, D_MODEL), lambda i: (i, 0)),
                   pl.BlockSpec((tm, N_EXPERTS), lambda i: (i, 0))],
        out_shape=[jax.ShapeDtypeStruct((m, D_MODEL), BF16),
                   jax.ShapeDtypeStruct((m, N_EXPERTS), F32)],
        compiler_params=_params(1),
        name="moe_router",
    )(x, g, scale, shift, w_router)


def _ffn_kernel(xs_ref, wg_ref, wu_ref, wd_ref, gt_ref, o_ref):
    f = pl.program_id(1)
    xs = xs_ref[...]
    a = _bdot(xs, wg_ref[...].astype(BF16))
    b = _bdot(xs, wu_ref[...].astype(BF16))
    hmid = (a * jax.nn.sigmoid(a) * b).astype(BF16)
    y = _bdot(hmid, wd_ref[...].astype(BF16))

    @pl.when(f == 0)
    def _():
        o_ref[...] = y

    @pl.when(f > 0)
    def _():
        o_ref[...] += y

    @pl.when(f == pl.num_programs(1) - 1)
    def _():
        o_ref[...] = o_ref[...] * gt_ref[...]


def expert_ffn(xs, w_gate, w_up, w_down, gates, *, tf):
    n_exp, cap, _ = xs.shape
    return pl.pallas_call(
        _ffn_kernel,
        grid=(n_exp, EXPERT_FF // tf),
        in_specs=[
            pl.BlockSpec((None, cap, D_MODEL), lambda e, f: (e, 0, 0)),
            pl.BlockSpec((None, D_MODEL, tf), lambda e, f: (e, 0, f)),
            pl.BlockSpec((None, D_MODEL, tf), lambda e, f: (e, 0, f)),
            pl.BlockSpec((None, tf, D_MODEL), lambda e, f: (e, f, 0)),
            pl.BlockSpec((None, cap, 1), lambda e, f: (e, 0, 0)),
        ],
        out_specs=pl.BlockSpec((None, cap, D_MODEL), lambda e, f: (e, 0, 0)),
        out_shape=jax.ShapeDtypeStruct((n_exp, cap, D_MODEL), F32),
        compiler_params=_params(2),
        name="expert_ffn",
    )(xs, w_gate, w_up, w_down, gates)


def moe_layer(x, g, scale, shift, gate, w_router, w_gate, w_up, w_down):
    hn, aff = moe_router(x, g, scale, shift, w_router, tm=512)
    gates, idx = lax.top_k(aff.T, EXPERT_CAP)
    xs = jnp.take(hn, idx.reshape(-1), axis=0).reshape(N_EXPERTS, EXPERT_CAP, D_MODEL)
    y = expert_ffn(xs, w_gate, w_up, w_down, gates[..., None], tf=256)
    comb = jnp.zeros((SEQ, D_MODEL), F32).at[idx.reshape(-1)].add(y.reshape(-1, D_MODEL))
    return x + (1.0 + gate) * comb


def _rope_tables(positions, dim):
    half = dim // 2
    inv = ROPE_THETA ** (-jnp.arange(half, dtype=F32) / half)
    ang = positions.astype(F32)[:, None] * inv
    return jnp.cos(ang), jnp.sin(ang)


def _row(v):
    return v.reshape(1, -1)


def conv_mixer(xs, g, shift, scale, gate, w_in, dw, ln_g, ln_b, w_out):
    u = norm_mod_matmul_glu(xs, g, scale, shift, w_in, tm=1024, tn=512, out_dtype=F32)
    v = conv_ln_swish(u, dw, _row(ln_g), _row(ln_b), tr=256)
    return matmul_residual(v, w_out, xs, gate, tm=1024, tn=512)


def pool_mixer(xs, g, shift, scale, gate, w_in, w_grp, ch_scale, w_out):
    u = norm_mod_matmul(xs, g, scale, shift, w_in, tm=1024, tn=512, out_dtype=F32)
    v = pool_mix(u, w_grp, _row(ch_scale), tr=256)
    return matmul_residual(v, w_out, xs, gate, tm=1024, tn=512)


def dil_mixer(xs, g, shift, scale, gate, pos, w_in, q_norm, k_norm, w_out):
    cos, sin = _rope_tables(pos, DIL_HEAD_DIM)
    cos_f = jnp.concatenate([cos, cos], axis=1)
    sin_f = jnp.concatenate([-sin, sin], axis=1)
    ones = jnp.ones((DIL_HEAD_DIM,), F32)
    nw = jnp.stack([w for grp in range(len(DIL_PATTERNS))
                    for w in (q_norm[grp], k_norm[grp], ones)])[:, None, :]
    qscale = jnp.asarray([DIL_HEAD_DIM ** -0.5, 1.0, 1.0] * len(DIL_PATTERNS), F32)
    qs = jnp.broadcast_to(qscale[:, None, None], nw.shape)
    qkv = norm_mod_matmul_dil(xs, g, scale, shift, w_in, nw, qs, cos_f, sin_f, tm=1024)
    parts = [dil_group_attention(qkv, grp, dil, tq=128) for grp, (_, dil) in enumerate(DIL_PATTERNS)]
    o = dil_merge(parts, tm=256)
    return matmul_residual(o, w_out, xs, gate, tm=1024, tn=512)


def mla_mixer(xs, g, shift, scale, gate, pos, w_in, q_a_norm, w_q_up, kv_a_norm, w_kv_up,
              q_norm, k_norm, w_out):
    cos, sin = _rope_tables(pos, MLA_ROPE)
    z32 = jnp.zeros_like(sin)
    z64 = jnp.zeros((SEQ, 64), F32)
    cos_p = jnp.concatenate([cos, cos, z64], axis=1)
    sin_a = jnp.concatenate([-sin, z32, z64], axis=1)
    sin_b = jnp.concatenate([z32, sin, z64], axis=1)
    w_in_pad = jnp.pad(w_in, ((0, 0), (0, 64)))
    wq_pad = jnp.pad(w_q_up.reshape(MLA_Q_RANK, MLA_HEADS, MLA_QK),
                     ((0, 0), (0, 0), (0, MLA_PAD - MLA_QK))).reshape(MLA_Q_RANK, MLA_HEADS * MLA_PAD)
    qn_pad = jnp.pad(q_norm, (0, MLA_PAD - MLA_QK)).reshape(1, MLA_PAD)
    kn_pad = jnp.pad(k_norm, (0, MLA_PAD - MLA_QK)).reshape(1, MLA_PAD)
    u = norm_mod_matmul(xs, g, scale, shift, w_in_pad, tm=512, tn=w_in_pad.shape[1], out_dtype=F32)
    q, k, v = mla_prep(u, _row(q_a_norm), _row(kv_a_norm), wq_pad, w_kv_up, qn_pad, kn_pad,
                       cos_p, sin_a, sin_b, tm=512)
    o = mla_attention(q, k, v, tq=512, tk=512)
    return matmul_residual(o, w_out, xs, gate, tm=1024, tn=512)


def kernel(x, c, positions, norm_g, ada_w, ada_b, conv_w_in, conv_dw, conv_ln_g, conv_ln_b, conv_w_out, pool_w_in, pool_w_grp, pool_scale, pool_w_out, dil_w_in, dil_q_norm, dil_k_norm, dil_w_out, mla_w_in, mla_q_a_norm, mla_w_q_up, mla_kv_a_norm, mla_w_kv_up, mla_q_norm, mla_k_norm, mla_w_out, moe_router, moe_w_gate, moe_w_up, moe_w_down):
    assert x.shape == (1, SEQ, D_MODEL) and ada_w.shape[0] == DEPTH
    xs = x.reshape(SEQ, D_MODEL)
    pos = positions.reshape(SEQ)
    mod = ada_all(c, ada_w, ada_b)

    def modulation(layer, sub):
        mrow = mod[2 * layer + sub]
        return mrow[:, 0:D_MODEL], mrow[:, D_MODEL:2 * D_MODEL], mrow[:, 2 * D_MODEL:]

    for layer in range(DEPTH):
        mixer = layer % 4
        occ = layer // 4
        shift, scale, gate = modulation(layer, 0)
        g = _row(norm_g[layer, 0])
        if mixer == 0:
            xs = conv_mixer(xs, g, shift, scale, gate, conv_w_in[occ], conv_dw[occ], conv_ln_g[occ],
                            conv_ln_b[occ], conv_w_out[occ])
        elif mixer == 1:
            xs = pool_mixer(xs, g, shift, scale, gate, pool_w_in[occ], pool_w_grp[occ], pool_scale[occ],
                            pool_w_out[occ])
        elif mixer == 2:
            xs = dil_mixer(xs, g, shift, scale, gate, pos, dil_w_in[occ], dil_q_norm[occ], dil_k_norm[occ],
                           dil_w_out[occ])
        else:
            xs = mla_mixer(xs, g, shift, scale, gate, pos, mla_w_in[occ], mla_q_a_norm[occ], mla_w_q_up[occ],
                           mla_kv_a_norm[occ], mla_w_kv_up[occ], mla_q_norm[occ], mla_k_norm[occ],
                           mla_w_out[occ])
        shift, scale, gate = modulation(layer, 1)
        xs = moe_layer(xs, _row(norm_g[layer, 1]), scale, shift, gate, moe_router[layer],
                       moe_w_gate[layer], moe_w_up[layer], moe_w_down[layer])
    return xs.reshape(1, SEQ, D_MODEL)
```

```python
import functools

import jax
import jax.numpy as jnp
from jax import lax
from jax.experimental import pallas as pl
from jax.experimental.pallas import tpu as pltpu

F32 = jnp.float32
BF16 = jnp.bfloat16

D_MODEL = 2048
SEQ = 8192
DEPTH = 4
EPS = 1e-6
ROPE_THETA = 10000.0
LOG2_E = 1.4426950408889634

CONV_WIDTH = 31
CONV_HALO = 16
POOL_WINDOWS = (2, 4, 8, 16)
POOL_GROUP_DIM = D_MODEL // len(POOL_WINDOWS)
POOL_HALO = 8

DIL_PATTERNS = ((128, 1), (512, 4), (2048, 16))
DIL_HEADS = 8
DIL_HEAD_DIM = 128
DIL_COLS = DIL_HEADS * DIL_HEAD_DIM
DIL_HALF = 64

MLA_HEADS = 16
MLA_Q_RANK = 512
MLA_KV_RANK = 512
MLA_NOPE = 128
MLA_ROPE = 64
MLA_V = 128
MLA_QK = MLA_NOPE + MLA_ROPE
MLA_PAD = 256

N_EXPERTS = 16
EXPERT_FF = 1024
EXPERT_CAP = 2 * SEQ // N_EXPERTS

VMEM_LIMIT_BYTES = 56 * 1024 * 1024


def _params(n_axes):
    return pltpu.CompilerParams(
        dimension_semantics=("arbitrary",) * n_axes, vmem_limit_bytes=VMEM_LIMIT_BYTES)


def _bdot(a, b):
    return jnp.dot(a, b, preferred_element_type=F32)


def _bdot_t(a, b):
    return lax.dot_general(a, b, (((1,), (1,)), ((), ())), preferred_element_type=F32)


def _ada_kernel(c_ref, w_ref, b_ref, o_ref):
    c = c_ref[...]
    cond = c * jax.nn.sigmoid(c)
    o_ref[...] = _bdot(cond.astype(BF16), w_ref[...].astype(BF16)) + b_ref[...]


def ada_all(c, ada_w, ada_b):
    n = ada_w.shape[0] * ada_w.shape[1]
    w = ada_w.reshape(n, D_MODEL, 3 * D_MODEL)
    b = ada_b.reshape(n, 1, 3 * D_MODEL)
    c8 = jnp.broadcast_to(c.reshape(1, D_MODEL), (8, D_MODEL))
    tn = 1024
    out = pl.pallas_call(
        _ada_kernel,
        grid=(n, 3 * D_MODEL // tn),
        in_specs=[
            pl.BlockSpec((8, D_MODEL), lambda l, j: (0, 0)),
            pl.BlockSpec((None, D_MODEL, tn), lambda l, j: (l, 0, j)),
            pl.BlockSpec((None, 1, tn), lambda l, j: (l, 0, j)),
        ],
        out_specs=pl.BlockSpec((None, 8, tn), lambda l, j: (l, 0, j)),
        out_shape=jax.ShapeDtypeStruct((n, 8, 3 * D_MODEL), F32),
        compiler_params=_params(2),
        name="ada_all",
    )(c8, w, b)
    return out[:, 0:1, :]


def _norm_mod(x, g, scale, shift):
    ms = jnp.mean(x * x, axis=-1, keepdims=True)
    y = x * lax.rsqrt(ms + EPS) * g
    return y * (1.0 + scale) + shift


def _nmm_plain_kernel(x_ref, g_ref, sc_ref, sh_ref, w_ref, o_ref, hn_ref):
    @pl.when(pl.program_id(1) == 0)
    def _():
        hn_ref[...] = _norm_mod(x_ref[...], g_ref[...], sc_ref[...], sh_ref[...]).astype(BF16)

    o_ref[...] = _bdot(hn_ref[...], w_ref[...].astype(BF16)).astype(o_ref.dtype)


def _nmm_glu_kernel(x_ref, g_ref, sc_ref, sh_ref, wa_ref, wb_ref, o_ref, hn_ref):
    @pl.when(pl.program_id(1) == 0)
    def _():
        hn_ref[...] = _norm_mod(x_ref[...], g_ref[...], sc_ref[...], sh_ref[...]).astype(BF16)

    hn = hn_ref[...]
    a = _bdot(hn, wa_ref[...].astype(BF16))
    b = _bdot(hn, wb_ref[...].astype(BF16))
    o_ref[...] = (a * jax.nn.sigmoid(b)).astype(o_ref.dtype)


def _nmm_dil_kernel(x_ref, g_ref, sc_ref, sh_ref, w_ref, nw_ref, qs_ref, cos_ref, sin_ref,
                    o_ref, hn_ref, stage_ref, *, dil):
    j = pl.program_id(1)

    @pl.when(j == 0)
    def _():
        hn_ref[...] = _norm_mod(x_ref[...], g_ref[...], sc_ref[...], sh_ref[...]).astype(BF16)

    acc = _bdot(hn_ref[...], w_ref[...].astype(BF16))
    is_v = j == 2

    @pl.when(is_v)
    def _():
        for h in range(DIL_HEADS):
            stage_ref[h] = acc[:, h * DIL_HEAD_DIM:(h + 1) * DIL_HEAD_DIM]

    @pl.when(jnp.logical_not(is_v))
    def _():
        nw = nw_ref[...] * qs_ref[...]
        cos = cos_ref[...]
        sin = sin_ref[...]
        for h in range(DIL_HEADS):
            a = acc[:, h * DIL_HEAD_DIM:(h + 1) * DIL_HEAD_DIM]
            ms = jnp.mean(a * a, axis=-1, keepdims=True)
            an = a * lax.rsqrt(ms + EPS) * nw
            stage_ref[h] = an * cos + pltpu.roll(an, DIL_HEAD_DIM // 2, 1) * sin

    rows = stage_ref.shape[1] // dil
    for p in range(dil):
        for h in range(DIL_HEADS):
            lo = p * DIL_COLS + h * DIL_HEAD_DIM
            src = stage_ref[h] if dil == 1 else stage_ref[h, pl.ds(p, rows, stride=dil), :]
            o_ref[:, lo:lo + DIL_HEAD_DIM] = src.astype(o_ref.dtype)


def _nmm_common_specs(tm):
    return [
        pl.BlockSpec((tm, D_MODEL), lambda i, j: (i, 0), pipeline_mode=pl.Buffered(1)),
        pl.BlockSpec((1, D_MODEL), lambda i, j: (0, 0)),
        pl.BlockSpec((1, D_MODEL), lambda i, j: (0, 0)),
        pl.BlockSpec((1, D_MODEL), lambda i, j: (0, 0)),
    ]


def norm_mod_matmul(x, g, scale, shift, w, *, tm, tn, out_dtype):
    m, n = x.shape[0], w.shape[1]
    return pl.pallas_call(
        _nmm_plain_kernel,
        grid=(m // tm, n // tn),
        in_specs=_nmm_common_specs(tm) + [pl.BlockSpec((D_MODEL, tn), lambda i, j: (0, j))],
        out_specs=pl.BlockSpec((tm, tn), lambda i, j: (i, j)),
        out_shape=jax.ShapeDtypeStruct((m, n), out_dtype),
        scratch_shapes=[pltpu.VMEM((tm, D_MODEL), BF16)],
        compiler_params=_params(2),
        name="norm_mod_matmul",
    )(x, g, scale, shift, w)


def norm_mod_matmul_glu(x, g, scale, shift, w, *, tm, tn, out_dtype):
    m, n = x.shape[0], w.shape[1] // 2
    nj = n // tn
    return pl.pallas_call(
        _nmm_glu_kernel,
        grid=(m // tm, nj),
        in_specs=_nmm_common_specs(tm) + [
            pl.BlockSpec((D_MODEL, tn), lambda i, j: (0, j)),
            pl.BlockSpec((D_MODEL, tn), lambda i, j: (0, j + nj)),
        ],
        out_specs=pl.BlockSpec((tm, tn), lambda i, j: (i, j)),
        out_shape=jax.ShapeDtypeStruct((m, n), out_dtype),
        scratch_shapes=[pltpu.VMEM((tm, D_MODEL), BF16)],
        compiler_params=_params(2),
        name="norm_mod_matmul_glu",
    )(x, g, scale, shift, w, w)


def norm_mod_matmul_dil(x, g, scale, shift, w, nw, qs, cos, sin, *, group, dil, tm):
    m = x.shape[0]
    tn = DIL_COLS
    return pl.pallas_call(
        functools.partial(_nmm_dil_kernel, dil=dil),
        grid=(m // tm, 3),
        in_specs=_nmm_common_specs(tm) + [
            pl.BlockSpec((D_MODEL, tn), lambda i, j: (0, 3 * group + j)),
            pl.BlockSpec((None, 1, DIL_HEAD_DIM), lambda i, j: (3 * group + j, 0, 0)),
            pl.BlockSpec((None, 1, DIL_HEAD_DIM), lambda i, j: (3 * group + j, 0, 0)),
            pl.BlockSpec((tm, DIL_HEAD_DIM), lambda i, j: (i, 0)),
            pl.BlockSpec((tm, DIL_HEAD_DIM), lambda i, j: (i, 0)),
        ],
        out_specs=pl.BlockSpec((tm // dil, dil * tn), lambda i, j: (i, j)),
        out_shape=jax.ShapeDtypeStruct((m // dil, 3 * dil * tn), BF16),
        scratch_shapes=[pltpu.VMEM((tm, D_MODEL), BF16),
                        pltpu.VMEM((DIL_HEADS, tm, DIL_HEAD_DIM), F32)],
        compiler_params=_params(2),
        name=f"norm_mod_matmul_dil_g{group}",
    )(x, g, scale, shift, w, nw, qs, cos, sin)


def _mmres_kernel(a_ref, w_ref, x_ref, gate_ref, o_ref):
    y = _bdot(a_ref[...], w_ref[...].astype(BF16))
    o_ref[...] = x_ref[...] + (1.0 + gate_ref[...]) * y


def matmul_residual(a, w, x, gate, *, tm, tn):
    m, k = a.shape
    n = w.shape[1]
    return pl.pallas_call(
        _mmres_kernel,
        grid=(m // tm, n // tn),
        in_specs=[
            pl.BlockSpec((tm, k), lambda i, j: (i, 0)),
            pl.BlockSpec((k, tn), lambda i, j: (0, j)),
            pl.BlockSpec((tm, tn), lambda i, j: (i, j)),
            pl.BlockSpec((1, tn), lambda i, j: (0, j)),
        ],
        out_specs=pl.BlockSpec((tm, tn), lambda i, j: (i, j)),
        out_shape=jax.ShapeDtypeStruct((m, n), F32),
        compiler_params=_params(2),
        name="matmul_residual",
    )(a, w, x, gate)


def _conv_kernel(prev_ref, cur_ref, next_ref, dw_ref, lg_ref, lb_ref, o_ref, ext_ref, cv_ref, *, tr):
    i = pl.program_id(0)
    last = pl.num_programs(0) - 1
    ext_ref[0:CONV_HALO, :] = jnp.where(i > 0, prev_ref[...], 0.0)
    ext_ref[CONV_HALO:CONV_HALO + tr, :] = cur_ref[...]
    ext_ref[CONV_HALO + tr:, :] = jnp.where(i < last, next_ref[...], 0.0)
    base = CONV_HALO - CONV_WIDTH // 2
    n_chunks = D_MODEL // 128

    def conv_chunk(c, s1):
        cs = pl.ds(pl.multiple_of(c * 128, 128), 128)
        acc = jnp.zeros((tr, 128), F32)
        for k in range(CONV_WIDTH):
            acc = acc + ext_ref[base + k:base + k + tr, cs] * dw_ref[k:k + 1, cs]
        cv_ref[:, cs] = acc
        return s1 + jnp.sum(acc, axis=1, keepdims=True)

    s1 = lax.fori_loop(0, n_chunks, conv_chunk, jnp.zeros((tr, 1), F32))
    mu = s1 * (1.0 / D_MODEL)

    def var_chunk(c, s2):
        cs = pl.ds(pl.multiple_of(c * 128, 128), 128)
        dlt = cv_ref[:, cs] - mu
        return s2 + jnp.sum(dlt * dlt, axis=1, keepdims=True)

    s2 = lax.fori_loop(0, n_chunks, var_chunk, jnp.zeros((tr, 1), F32))
    rstd = lax.rsqrt(s2 * (1.0 / D_MODEL) + EPS)

    def out_chunk(c, carry):
        cs = pl.ds(pl.multiple_of(c * 128, 128), 128)
        un = (cv_ref[:, cs] - mu) * rstd * lg_ref[:, cs] + lb_ref[:, cs]
        o_ref[:, cs] = (un * jax.nn.sigmoid(un)).astype(o_ref.dtype)
        return carry

    lax.fori_loop(0, n_chunks, out_chunk, 0)


def conv_ln_swish(u, dw, ln_g, ln_b, *, tr):
    m = u.shape[0]
    hb = tr // CONV_HALO
    n_hb = m // CONV_HALO
    return pl.pallas_call(
        functools.partial(_conv_kernel, tr=tr),
        grid=(m // tr,),
        in_specs=[
            pl.BlockSpec((CONV_HALO, D_MODEL), lambda i: (jnp.maximum(i * hb - 1, 0), 0)),
            pl.BlockSpec((tr, D_MODEL), lambda i: (i, 0)),
            pl.BlockSpec((CONV_HALO, D_MODEL), lambda i: (jnp.minimum((i + 1) * hb, n_hb - 1), 0)),
            pl.BlockSpec((CONV_WIDTH, D_MODEL), lambda i: (0, 0)),
            pl.BlockSpec((1, D_MODEL), lambda i: (0, 0)),
            pl.BlockSpec((1, D_MODEL), lambda i: (0, 0)),
        ],
        out_specs=pl.BlockSpec((tr, D_MODEL), lambda i: (i, 0)),
        out_shape=jax.ShapeDtypeStruct((m, D_MODEL), BF16),
        scratch_shapes=[
            pltpu.VMEM((tr + 2 * CONV_HALO, D_MODEL), F32),
            pltpu.VMEM((tr, D_MODEL), F32),
        ],
        compiler_params=_params(1),
        name="conv_ln_swish",
    )(u, u, u, dw, ln_g, ln_b)


def _pool_kernel(prev_ref, cur_ref, next_ref, wg_ref, cs_ref, o_ref, ext_ref, mix_ref, *, tr):
    i = pl.program_id(0)
    last = pl.num_programs(0) - 1
    ext_ref[0:POOL_HALO, :] = jnp.where(i > 0, prev_ref[...], 0.0)
    ext_ref[POOL_HALO:POOL_HALO + tr, :] = cur_ref[...]
    ext_ref[POOL_HALO + tr:, :] = jnp.where(i < last, next_ref[...], 0.0)
    t = i * tr + lax.broadcasted_iota(jnp.int32, (tr, 1), 0)
    for g, window in enumerate(POOL_WINDOWS):
        r = window // 2
        cnt = (jnp.minimum(t + r + 1, SEQ) - jnp.maximum(t - r, 0)).astype(F32)
        inv_cnt = 1.0 / cnt
        for c in range(POOL_GROUP_DIM // 128):
            lo = g * POOL_GROUP_DIM + c * 128
            acc = jnp.zeros((tr, 128), F32)
            for dlt in range(-r, r + 1):
                acc = acc + ext_ref[POOL_HALO + dlt:POOL_HALO + dlt + tr, lo:lo + 128]
            mix = acc * inv_cnt - ext_ref[POOL_HALO:POOL_HALO + tr, lo:lo + 128]
            mix_ref[:, lo:lo + 128] = mix.astype(BF16)
    for g in range(len(POOL_WINDOWS)):
        cols = slice(g * POOL_GROUP_DIM, (g + 1) * POOL_GROUP_DIM)
        y = _bdot(mix_ref[:, cols], wg_ref[g].astype(BF16))
        o_ref[:, cols] = (y * cs_ref[:, cols]).astype(o_ref.dtype)


def pool_mix(u, w_grp, ch_scale, *, tr):
    m = u.shape[0]
    hb = tr // POOL_HALO
    n_hb = m // POOL_HALO
    n_grp = len(POOL_WINDOWS)
    return pl.pallas_call(
        functools.partial(_pool_kernel, tr=tr),
        grid=(m // tr,),
        in_specs=[
            pl.BlockSpec((POOL_HALO, D_MODEL), lambda i: (jnp.maximum(i * hb - 1, 0), 0)),
            pl.BlockSpec((tr, D_MODEL), lambda i: (i, 0)),
            pl.BlockSpec((POOL_HALO, D_MODEL), lambda i: (jnp.minimum((i + 1) * hb, n_hb - 1), 0)),
            pl.BlockSpec((n_grp, POOL_GROUP_DIM, POOL_GROUP_DIM), lambda i: (0, 0, 0)),
            pl.BlockSpec((1, D_MODEL), lambda i: (0, 0)),
        ],
        out_specs=pl.BlockSpec((tr, D_MODEL), lambda i: (i, 0)),
        out_shape=jax.ShapeDtypeStruct((m, D_MODEL), BF16),
        scratch_shapes=[
            pltpu.VMEM((tr + 2 * POOL_HALO, D_MODEL), F32),
            pltpu.VMEM((tr, D_MODEL), BF16),
        ],
        compiler_params=_params(1),
        name="pool_mix",
    )(u, u, u, w_grp, ch_scale)


def _dil_attn_kernel(q_ref, kp_ref, kc_ref, kn_ref, vp_ref, vc_ref, vn_ref,
                     o_ref, m_ref, l_ref, *, tq, n_slots):
    i = pl.program_id(1)
    row = lax.broadcasted_iota(jnp.int32, (tq, 3 * tq), 0)
    col = lax.broadcasted_iota(jnp.int32, (tq, 3 * tq), 1)
    key_slot = i * tq - tq + col
    valid = (jnp.abs(col - tq - row) <= DIL_HALF) & (key_slot >= 0) & (key_slot < n_slots)
    for h in range(DIL_HEADS):
        hs = slice(h * DIL_HEAD_DIM, (h + 1) * DIL_HEAD_DIM)
        q = q_ref[:, hs]
        s = jnp.concatenate(
            [_bdot_t(q, kp_ref[:, hs]), _bdot_t(q, kc_ref[:, hs]), _bdot_t(q, kn_ref[:, hs])], axis=1)
        s = jnp.where(valid, s, -jnp.inf)
        m = jnp.max(s, axis=1, keepdims=True)
        p = jnp.exp(s - m)
        l = jnp.sum(p, axis=1, keepdims=True)
        pb = p.astype(BF16)
        o = (_bdot(pb[:, 0:tq], vp_ref[:, hs]) + _bdot(pb[:, tq:2 * tq], vc_ref[:, hs])
             + _bdot(pb[:, 2 * tq:], vn_ref[:, hs]))
        o_ref[:, hs] = o
        m_ref[:, hs] = jnp.broadcast_to(m, (tq, DIL_HEAD_DIM))
        l_ref[:, hs] = jnp.broadcast_to(l, (tq, DIL_HEAD_DIM))


def dil_group_attention(qkv, group, dil, *, tq):
    n_slots = SEQ // dil
    nb = n_slots // tq

    def spec(slab, shift):
        return pl.BlockSpec(
            (tq, DIL_COLS), lambda p, i: (jnp.clip(i + shift, 0, nb - 1), slab * dil + p))

    out_spec = pl.BlockSpec((tq, DIL_COLS), lambda p, i: (i, p))
    out_sds = jax.ShapeDtypeStruct((n_slots, dil * DIL_COLS), F32)
    return pl.pallas_call(
        functools.partial(_dil_attn_kernel, tq=tq, n_slots=n_slots),
        grid=(dil, nb),
        in_specs=[spec(0, 0),
                  spec(1, -1), spec(1, 0), spec(1, 1),
                  spec(2, -1), spec(2, 0), spec(2, 1)],
        out_specs=[out_spec, out_spec, out_spec],
        out_shape=[out_sds, out_sds, out_sds],
        compiler_params=_params(2),
        name=f"dil_attn_g{group}",
    )(qkv, qkv, qkv, qkv, qkv, qkv, qkv)


def _dil_merge_kernel(*refs, dils):
    n_in = 3 * len(dils)
    in_refs, out_ref, scratch = refs[:n_in], refs[n_in], list(refs[n_in + 1:])
    tm = out_ref.shape[0]
    for h in range(DIL_HEADS):
        tok = []
        for gi, dil in enumerate(dils):
            for a, src in enumerate(in_refs[3 * gi:3 * gi + 3]):
                if dil == 1:
                    tok.append(src[:, h * DIL_HEAD_DIM:(h + 1) * DIL_HEAD_DIM])
                    continue
                dst = scratch.pop(0)
                for p in range(dil):
                    lo = p * DIL_COLS + h * DIL_HEAD_DIM
                    dst[h, pl.ds(p, tm // dil, stride=dil), :] = src[:, lo:lo + DIL_HEAD_DIM]
                tok.append(dst[h])
                scratch.append(dst)
        o0, m0, l0, o1, m1, l1, o2, m2, l2 = tok
        mm = jnp.maximum(jnp.maximum(m0, m1), m2)
        w0 = jnp.exp(m0 - mm)
        w1 = jnp.exp(m1 - mm)
        w2 = jnp.exp(m2 - mm)
        num = w0 * o0 + w1 * o1 + w2 * o2
        den = w0 * l0 + w1 * l1 + w2 * l2
        out_ref[:, h * DIL_HEAD_DIM:(h + 1) * DIL_HEAD_DIM] = (num / den).astype(out_ref.dtype)


def dil_merge(parts, dils, *, tm):
    in_specs = []
    n_scratch = 0
    for dil in dils:
        in_specs += [pl.BlockSpec((tm // dil, dil * DIL_COLS), lambda i: (i, 0))] * 3
        n_scratch += 3 if dil > 1 else 0
    flat = [a for part in parts for a in part]
    return pl.pallas_call(
        functools.partial(_dil_merge_kernel, dils=tuple(dils)),
        grid=(SEQ // tm,),
        in_specs=in_specs,
        out_specs=pl.BlockSpec((tm, DIL_COLS), lambda i: (i, 0)),
        out_shape=jax.ShapeDtypeStruct((SEQ, DIL_COLS), BF16),
        scratch_shapes=[pltpu.VMEM((DIL_HEADS, tm, DIL_HEAD_DIM), F32)] * n_scratch,
        compiler_params=_params(1),
        name="dil_merge",
    )(*flat)


def _mla_prep_kernel(u_ref, qan_ref, kvan_ref, wq_ref, wkv_ref, qn_ref, kn_ref,
                     cos_ref, sa_ref, sb_ref, q_out, k_out, vt_out, cq_s, ckv_s):
    @pl.when(pl.program_id(1) == 0)
    def _():
        cq = u_ref[:, 0:MLA_Q_RANK]
        cq_s[...] = (cq * lax.rsqrt(jnp.mean(cq * cq, axis=-1, keepdims=True) + EPS)
                     * qan_ref[...]).astype(BF16)
        ckv = u_ref[:, MLA_Q_RANK:MLA_Q_RANK + MLA_KV_RANK]
        ckv_s[...] = (ckv * lax.rsqrt(jnp.mean(ckv * ckv, axis=-1, keepdims=True) + EPS)
                      * kvan_ref[...]).astype(BF16)

    cos = cos_ref[...]
    sa = sa_ref[...]
    sb = sb_ref[...]

    def norm_rope(nope, rp, w_ref):
        ssq = jnp.sum(nope * nope, axis=-1, keepdims=True) + jnp.sum(rp * rp, axis=-1, keepdims=True)
        r = lax.rsqrt(ssq * (1.0 / MLA_QK) + EPS)
        nope_n = nope * r * w_ref[:, 0:128]
        rp_n = rp * r * w_ref[:, 128:256]
        rp_r = rp_n * cos + pltpu.roll(rp_n, 96, 1) * sa + pltpu.roll(rp_n, 32, 1) * sb
        return nope_n, rp_r

    q = _bdot(cq_s[...], wq_ref[...].astype(BF16))
    kv = _bdot(ckv_s[...], wkv_ref[...].astype(BF16))
    kr = u_ref[:, MLA_Q_RANK + MLA_KV_RANK:MLA_Q_RANK + MLA_KV_RANK + 128]
    scale = MLA_QK ** -0.5 * LOG2_E
    qn, qr = norm_rope(q[:, 0:128], q[:, 128:256], qn_ref)
    q_out[:, 0:128] = (qn * scale).astype(BF16)
    q_out[:, 128:256] = (qr * scale).astype(BF16)
    kn, kr_r = norm_rope(kv[:, 0:128], kr, kn_ref)
    k_out[:, 0:128] = kn.astype(BF16)
    k_out[:, 128:256] = kr_r.astype(BF16)
    vt_out[...] = kv[:, 128:256].T.astype(BF16)


def mla_prep(u, q_a_norm, kv_a_norm, wq_pad, w_kv_up, qn_pad, kn_pad, cos, sa, sb, *, tm):
    m = u.shape[0]
    ucols = u.shape[1]
    vec = lambda n: pl.BlockSpec((1, n), lambda i, h: (0, 0))
    tab = pl.BlockSpec((tm, 128), lambda i, h: (i, 0))
    head_out = lambda w: pl.BlockSpec((None, tm, w), lambda i, h: (h, i, 0))
    return pl.pallas_call(
        _mla_prep_kernel,
        grid=(m // tm, MLA_HEADS),
        in_specs=[
            pl.BlockSpec((tm, ucols), lambda i, h: (i, 0)),
            vec(MLA_Q_RANK), vec(MLA_KV_RANK),
            pl.BlockSpec((MLA_Q_RANK, MLA_PAD), lambda i, h: (0, h)),
            pl.BlockSpec((MLA_KV_RANK, MLA_NOPE + MLA_V), lambda i, h: (0, h)),
            vec(MLA_PAD), vec(MLA_PAD),
            tab, tab, tab,
        ],
        out_specs=[head_out(MLA_PAD), head_out(MLA_PAD),
                   pl.BlockSpec((None, MLA_V, tm), lambda i, h: (h, 0, i))],
        out_shape=[
            jax.ShapeDtypeStruct((MLA_HEADS, m, MLA_PAD), BF16),
            jax.ShapeDtypeStruct((MLA_HEADS, m, MLA_PAD), BF16),
            jax.ShapeDtypeStruct((MLA_HEADS, MLA_V, m), BF16),
        ],
        scratch_shapes=[pltpu.VMEM((tm, MLA_Q_RANK), BF16), pltpu.VMEM((tm, MLA_KV_RANK), BF16)],
        compiler_params=_params(2),
        name="mla_prep",
    )(u, q_a_norm, kv_a_norm, wq_pad, w_kv_up, qn_pad, kn_pad, cos, sa, sb)


def _mla_attn_kernel(q_ref, k_ref, vt_ref, o_ref, m_s, l_s, acc_s, *, tk):
    q = q_ref[...]
    m_s[...] = jnp.full(m_s.shape, -jnp.inf, F32)
    l_s[...] = jnp.zeros(l_s.shape, F32)
    acc_s[...] = jnp.zeros(acc_s.shape, F32)

    def body(c, carry):
        ks = pl.ds(pl.multiple_of(c * tk, tk), tk)
        st = _bdot_t(k_ref[ks, :], q)
        m_prev = m_s[...]
        m_new = jnp.maximum(m_prev, jnp.max(st, axis=0, keepdims=True))
        alpha = jnp.exp2(m_prev - m_new)
        pt = jnp.exp2(st - m_new)
        l_s[...] = alpha * l_s[...] + jnp.sum(pt, axis=0, keepdims=True)
        acc_s[...] = alpha * acc_s[...] + _bdot(vt_ref[:, ks], pt.astype(BF16))
        m_s[...] = m_new
        return carry

    lax.fori_loop(0, k_ref.shape[0] // tk, body, 0)
    o_ref[...] = (acc_s[...] / l_s[...]).T.astype(o_ref.dtype)


def mla_attention(q, k, vt, *, tq, tk):
    n_heads, m, _ = q.shape
    return pl.pallas_call(
        functools.partial(_mla_attn_kernel, tk=tk),
        grid=(n_heads, m // tq),
        in_specs=[
            pl.BlockSpec((None, tq, MLA_PAD), lambda h, i: (h, i, 0)),
            pl.BlockSpec((None, m, MLA_PAD), lambda h, i: (h, 0, 0)),
            pl.BlockSpec((None, MLA_V, m), lambda h, i: (h, 0, 0)),
        ],
        out_specs=pl.BlockSpec((tq, MLA_V), lambda h, i: (i, h)),
        out_shape=jax.ShapeDtypeStruct((m, n_heads * MLA_V), BF16),
        scratch_shapes=[
            pltpu.VMEM((1, tq), F32), pltpu.VMEM((1, tq), F32), pltpu.VMEM((MLA_V, tq), F32)],
        compiler_params=_params(2),
        name="mla_attention",
    )(q, k, vt)


def _router_kernel(x_ref, g_ref, sc_ref, sh_ref, wr_ref, hn_out, aff_out):
    hn = _norm_mod(x_ref[...], g_ref[...], sc_ref[...], sh_ref[...])
    hn_out[...] = hn.astype(BF16)
    logits = jnp.dot(hn, wr_ref[...], preferred_element_type=F32, precision=lax.Precision.HIGHEST)
    z = jnp.exp(logits - jnp.max(logits, axis=-1, keepdims=True))
    aff_out[...] = z / jnp.sum(z, axis=-1, keepdims=True)


def moe_router(x, g, scale, shift, w_router, *, tm):
    m = x.shape[0]
    vec = pl.BlockSpec((1, D_MODEL), lambda i: (0, 0))
    return pl.pallas_call(
        _router_kernel,
        grid=(m // tm,),
        in_specs=[pl.BlockSpec((tm, D_MODEL), lambda i: (i, 0)), vec, vec, vec,
                  pl.BlockSpec((D_MODEL, N_EXPERTS), lambda i: (0, 0))],
        out_specs=[pl.BlockSpec((tm, D_MODEL), lambda i: (i, 0)),
                   pl.BlockSpec((tm, N_EXPERTS), lambda i: (i, 0))],
        out_shape=[jax.ShapeDtypeStruct((m, D_MODEL), BF16),
                   jax.ShapeDtypeStruct((m, N_EXPERTS), F32)],
        compiler_params=_params(1),
        name="moe_router",
    )(x, g, scale, shift, w_router)


def _ffn_kernel(xs_ref, wg_ref, wu_ref, wd_ref, gt_ref, o_ref):
    f = pl.program_id(1)
    xs = xs_ref[...]
    a = _bdot(xs, wg_ref[...].astype(BF16))
    b = _bdot(xs, wu_ref[...].astype(BF16))
    hmid = (a * jax.nn.sigmoid(a) * b).astype(BF16)
    y = _bdot(hmid, wd_ref[...].astype(BF16))

    @pl.when(f == 0)
    def _():
        o_ref[...] = y

    @pl.when(f > 0)
    def _():
        o_ref[...] += y

    @pl.when(f == pl.num_programs(1) - 1)
    def _():
        o_ref[...] = o_ref[...] * gt_ref[...]


def expert_ffn(xs, w_gate, w_up, w_down, gates, *, tf):
    n_exp, cap, _ = xs.shape
    return pl.pallas_call(
        _ffn_kernel,
        grid=(n_exp, EXPERT_FF // tf),
        in_specs=[
            pl.BlockSpec((None, cap, D_MODEL), lambda e, f: (e, 0, 0)),
            pl.BlockSpec((None, D_MODEL, tf), lambda e, f: (e, 0, f)),
            pl.BlockSpec((None, D_MODEL, tf), lambda e, f: (e, 0, f)),
            pl.BlockSpec((None, tf, D_MODEL), lambda e, f: (e, f, 0)),
            pl.BlockSpec((None, cap, 1), lambda e, f: (e, 0, 0)),
        ],
        out_specs=pl.BlockSpec((None, cap, D_MODEL), lambda e, f: (e, 0, 0)),
        out_shape=jax.ShapeDtypeStruct((n_exp, cap, D_MODEL), F32),
        compiler_params=_params(2),
        name="expert_ffn",
    )(xs, w_gate, w_up, w_down, gates)


def moe_layer(x, g, scale, shift, gate, w_router, w_gate, w_up, w_down):
    hn, aff = moe_router(x, g, scale, shift, w_router, tm=512)
    gates, idx = lax.top_k(aff.T, EXPERT_CAP)
    xs = jnp.take(hn, idx.reshape(-1), axis=0).reshape(N_EXPERTS, EXPERT_CAP, D_MODEL)
    y = expert_ffn(xs, w_gate, w_up, w_down, gates[..., None], tf=256)
    comb = jnp.zeros((SEQ, D_MODEL), F32).at[idx.reshape(-1)].add(y.reshape(-1, D_MODEL))
    return x + (1.0 + gate) * comb


def _rope_tables(positions, dim):
    half = dim // 2
    inv = ROPE_THETA ** (-jnp.arange(half, dtype=F32) / half)
    ang = positions.astype(F32)[:, None] * inv
    return jnp.cos(ang), jnp.sin(ang)


def _row(v):
    return v.reshape(1, -1)


def conv_mixer(xs, g, shift, scale, gate, w_in, dw, ln_g, ln_b, w_out):
    u = norm_mod_matmul_glu(xs, g, scale, shift, w_in, tm=1024, tn=512, out_dtype=F32)
    v = conv_ln_swish(u, dw, _row(ln_g), _row(ln_b), tr=256)
    return matmul_residual(v, w_out, xs, gate, tm=1024, tn=512)


def pool_mixer(xs, g, shift, scale, gate, w_in, w_grp, ch_scale, w_out):
    u = norm_mod_matmul(xs, g, scale, shift, w_in, tm=1024, tn=512, out_dtype=F32)
    v = pool_mix(u, w_grp, _row(ch_scale), tr=256)
    return matmul_residual(v, w_out, xs, gate, tm=1024, tn=512)


def dil_mixer(xs, g, shift, scale, gate, pos, w_in, q_norm, k_norm, w_out):
    cos, sin = _rope_tables(pos, DIL_HEAD_DIM)
    cos_f = jnp.concatenate([cos, cos], axis=1)
    sin_f = jnp.concatenate([-sin, sin], axis=1)
    ones = jnp.ones((DIL_HEAD_DIM,), F32)
    nw = jnp.stack([w for grp in range(len(DIL_PATTERNS))
                    for w in (q_norm[grp], k_norm[grp], ones)])[:, None, :]
    qscale = jnp.asarray([DIL_HEAD_DIM ** -0.5, 1.0, 1.0] * len(DIL_PATTERNS), F32)
    qs = jnp.broadcast_to(qscale[:, None, None], nw.shape)
    dils = [dil for _, dil in DIL_PATTERNS]
    parts = []
    for grp, dil in enumerate(dils):
        qkv = norm_mod_matmul_dil(xs, g, scale, shift, w_in, nw, qs, cos_f, sin_f,
                                  group=grp, dil=dil, tm=1024)
        parts.append(dil_group_attention(qkv, grp, dil, tq=256))
    o = dil_merge(parts, dils, tm=256)
    return matmul_residual(o, w_out, xs, gate, tm=1024, tn=512)


def mla_mixer(xs, g, shift, scale, gate, pos, w_in, q_a_norm, w_q_up, kv_a_norm, w_kv_up,
              q_norm, k_norm, w_out):
    cos, sin = _rope_tables(pos, MLA_ROPE)
    z32 = jnp.zeros_like(sin)
    z64 = jnp.zeros((SEQ, 64), F32)
    cos_p = jnp.concatenate([cos, cos, z64], axis=1)
    sin_a = jnp.concatenate([-sin, z32, z64], axis=1)
    sin_b = jnp.concatenate([z32, sin, z64], axis=1)
    w_in_pad = jnp.pad(w_in, ((0, 0), (0, 64)))
    wq_pad = jnp.pad(w_q_up.reshape(MLA_Q_RANK, MLA_HEADS, MLA_QK),
                     ((0, 0), (0, 0), (0, MLA_PAD - MLA_QK))).reshape(MLA_Q_RANK, MLA_HEADS * MLA_PAD)
    qn_pad = jnp.pad(q_norm, (0, MLA_PAD - MLA_QK)).reshape(1, MLA_PAD)
    kn_pad = jnp.pad(k_norm, (0, MLA_PAD - MLA_QK)).reshape(1, MLA_PAD)
    u = norm_mod_matmul(xs, g, scale, shift, w_in_pad, tm=512, tn=w_in_pad.shape[1], out_dtype=F32)
    q, k, vt = mla_prep(u, _row(q_a_norm), _row(kv_a_norm), wq_pad, w_kv_up, qn_pad, kn_pad,
                       cos_p, sin_a, sin_b, tm=512)
    o = mla_attention(q, k, vt, tq=512, tk=4096)
    return matmul_residual(o, w_out, xs, gate, tm=1024, tn=512)


def kernel(x, c, positions, norm_g, ada_w, ada_b, conv_w_in, conv_dw, conv_ln_g, conv_ln_b, conv_w_out, pool_w_in, pool_w_grp, pool_scale, pool_w_out, dil_w_in, dil_q_norm, dil_k_norm, dil_w_out, mla_w_in, mla_q_a_norm, mla_w_q_up, mla_kv_a_norm, mla_w_kv_up, mla_q_norm, mla_k_norm, mla_w_out, moe_router, moe_w_gate, moe_w_up, moe_w_down):
    assert x.shape == (1, SEQ, D_MODEL) and ada_w.shape[0] == DEPTH
    xs = x.reshape(SEQ, D_MODEL)
    pos = positions.reshape(SEQ)
    mod = ada_all(c, ada_w, ada_b)

    def modulation(layer, sub):
        mrow = mod[2 * layer + sub]
        return mrow[:, 0:D_MODEL], mrow[:, D_MODEL:2 * D_MODEL], mrow[:, 2 * D_MODEL:]

    for layer in range(DEPTH):
        mixer = layer % 4
        occ = layer // 4
        shift, scale, gate = modulation(layer, 0)
        g = _row(norm_g[layer, 0])
        if mixer == 0:
            xs = conv_mixer(xs, g, shift, scale, gate, conv_w_in[occ], conv_dw[occ], conv_ln_g[occ],
                            conv_ln_b[occ], conv_w_out[occ])
        elif mixer == 1:
            xs = pool_mixer(xs, g, shift, scale, gate, pool_w_in[occ], pool_w_grp[occ], pool_scale[occ],
                            pool_w_out[occ])
        elif mixer == 2:
            xs = dil_mixer(xs, g, shift, scale, gate, pos, dil_w_in[occ], dil_q_norm[occ], dil_k_norm[occ],
                           dil_w_out[occ])
        else:
            xs = mla_mixer(xs, g, shift, scale, gate, pos, mla_w_in[occ], mla_q_a_norm[occ], mla_w_q_up[occ],
                           mla_kv_a_norm[occ], mla_w_kv_up[occ], mla_q_norm[occ], mla_k_norm[occ],
                           mla_w_out[occ])
        shift, scale, gate = modulation(layer, 1)
        xs = moe_layer(xs, _row(norm_g[layer, 1]), scale, shift, gate, moe_router[layer],
                       moe_w_gate[layer], moe_w_up[layer], moe_w_down[layer])
    return xs.reshape(1, SEQ, D_MODEL)
```

```python
import functools

import jax
import jax.numpy as jnp
from jax import lax
from jax.experimental import pallas as pl
from jax.experimental.pallas import tpu as pltpu

F32 = jnp.float32
BF16 = jnp.bfloat16

D_MODEL = 2048
SEQ = 8192
DEPTH = 4
EPS = 1e-6
ROPE_THETA = 10000.0
LOG2_E = 1.4426950408889634

CONV_WIDTH = 31
CONV_HALO = 16
CONV_ROWS = 128
POOL_WINDOWS = (2, 4, 8, 16)
POOL_GROUP_DIM = D_MODEL // len(POOL_WINDOWS)
POOL_HALO = 8

DIL_PATTERNS = ((128, 1), (512, 4), (2048, 16))
DIL_HEADS = 8
DIL_HEAD_DIM = 128
DIL_COLS = DIL_HEADS * DIL_HEAD_DIM
DIL_HALF = 64

MLA_HEADS = 16
MLA_Q_RANK = 512
MLA_KV_RANK = 512
MLA_NOPE = 128
MLA_ROPE = 64
MLA_V = 128
MLA_QK = MLA_NOPE + MLA_ROPE
MLA_PAD = 256

N_EXPERTS = 16
EXPERT_FF = 1024
EXPERT_CAP = 2 * SEQ // N_EXPERTS

VMEM_LIMIT_BYTES = 56 * 1024 * 1024


def _params(n_axes):
    return pltpu.CompilerParams(
        dimension_semantics=("arbitrary",) * n_axes, vmem_limit_bytes=VMEM_LIMIT_BYTES)


def _bdot(a, b):
    return jnp.dot(a, b, preferred_element_type=F32)


def _bdot_t(a, b):
    return lax.dot_general(a, b, (((1,), (1,)), ((), ())), preferred_element_type=F32)


def _ada_kernel(c_ref, w_ref, b_ref, o_ref):
    c = c_ref[...]
    cond = c * jax.nn.sigmoid(c)
    o_ref[...] = _bdot(cond.astype(BF16), w_ref[...].astype(BF16)) + b_ref[...]


def ada_all(c, ada_w, ada_b):
    n = ada_w.shape[0] * ada_w.shape[1]
    w = ada_w.reshape(n, D_MODEL, 3 * D_MODEL)
    b = ada_b.reshape(n, 1, 3 * D_MODEL)
    c8 = jnp.broadcast_to(c.reshape(1, D_MODEL), (8, D_MODEL))
    tn = 1024
    out = pl.pallas_call(
        _ada_kernel,
        grid=(n, 3 * D_MODEL // tn),
        in_specs=[
            pl.BlockSpec((8, D_MODEL), lambda l, j: (0, 0)),
            pl.BlockSpec((None, D_MODEL, tn), lambda l, j: (l, 0, j)),
            pl.BlockSpec((None, 1, tn), lambda l, j: (l, 0, j)),
        ],
        out_specs=pl.BlockSpec((None, 8, tn), lambda l, j: (l, 0, j)),
        out_shape=jax.ShapeDtypeStruct((n, 8, 3 * D_MODEL), F32),
        compiler_params=_params(2),
        name="ada_all",
    )(c8, w, b)
    return out[:, 0:1, :]


def _norm_mod(x, g, scale, shift):
    ms = jnp.mean(x * x, axis=-1, keepdims=True)
    y = x * lax.rsqrt(ms + EPS) * g
    return y * (1.0 + scale) + shift


def _nmm_plain_kernel(x_ref, g_ref, sc_ref, sh_ref, w_ref, o_ref, hn_ref):
    @pl.when(pl.program_id(1) == 0)
    def _():
        hn_ref[...] = _norm_mod(x_ref[...], g_ref[...], sc_ref[...], sh_ref[...]).astype(BF16)

    o_ref[...] = _bdot(hn_ref[...], w_ref[...].astype(BF16)).astype(o_ref.dtype)


def _nmm_glu_kernel(x_ref, g_ref, sc_ref, sh_ref, wa_ref, wb_ref, o_ref, hn_ref):
    @pl.when(pl.program_id(1) == 0)
    def _():
        hn_ref[...] = _norm_mod(x_ref[...], g_ref[...], sc_ref[...], sh_ref[...]).astype(BF16)

    hn = hn_ref[...]
    a = _bdot(hn, wa_ref[...].astype(BF16))
    b = _bdot(hn, wb_ref[...].astype(BF16))
    o_ref[...] = (a * jax.nn.sigmoid(b)).astype(o_ref.dtype)


def _nmm_dil_kernel(x_ref, g_ref, sc_ref, sh_ref, w_ref, nw_ref, qs_ref, cos_ref, sin_ref,
                    o_ref, hn_ref, stage_ref, *, dil):
    j = pl.program_id(1)

    @pl.when(j == 0)
    def _():
        hn_ref[...] = _norm_mod(x_ref[...], g_ref[...], sc_ref[...], sh_ref[...]).astype(BF16)

    acc = _bdot(hn_ref[...], w_ref[...].astype(BF16))
    is_v = j == 2

    @pl.when(is_v)
    def _():
        for h in range(DIL_HEADS):
            stage_ref[h] = acc[:, h * DIL_HEAD_DIM:(h + 1) * DIL_HEAD_DIM]

    @pl.when(jnp.logical_not(is_v))
    def _():
        nw = nw_ref[...] * qs_ref[...]
        cos = cos_ref[...]
        sin = sin_ref[...]
        for h in range(DIL_HEADS):
            a = acc[:, h * DIL_HEAD_DIM:(h + 1) * DIL_HEAD_DIM]
            ms = jnp.mean(a * a, axis=-1, keepdims=True)
            an = a * lax.rsqrt(ms + EPS) * nw
            stage_ref[h] = an * cos + pltpu.roll(an, DIL_HEAD_DIM // 2, 1) * sin

    rows = stage_ref.shape[1] // dil
    for p in range(dil):
        for h in range(DIL_HEADS):
            lo = p * DIL_COLS + h * DIL_HEAD_DIM
            src = stage_ref[h] if dil == 1 else stage_ref[h, pl.ds(p, rows, stride=dil), :]
            o_ref[:, lo:lo + DIL_HEAD_DIM] = src.astype(o_ref.dtype)


def _nmm_common_specs(tm):
    return [
        pl.BlockSpec((tm, D_MODEL), lambda i, j: (i, 0), pipeline_mode=pl.Buffered(1)),
        pl.BlockSpec((1, D_MODEL), lambda i, j: (0, 0)),
        pl.BlockSpec((1, D_MODEL), lambda i, j: (0, 0)),
        pl.BlockSpec((1, D_MODEL), lambda i, j: (0, 0)),
    ]


def norm_mod_matmul(x, g, scale, shift, w, *, tm, tn, out_dtype):
    m, n = x.shape[0], w.shape[1]
    return pl.pallas_call(
        _nmm_plain_kernel,
        grid=(m // tm, n // tn),
        in_specs=_nmm_common_specs(tm) + [pl.BlockSpec((D_MODEL, tn), lambda i, j: (0, j))],
        out_specs=pl.BlockSpec((tm, tn), lambda i, j: (i, j)),
        out_shape=jax.ShapeDtypeStruct((m, n), out_dtype),
        scratch_shapes=[pltpu.VMEM((tm, D_MODEL), BF16)],
        compiler_params=_params(2),
        name="norm_mod_matmul",
    )(x, g, scale, shift, w)


def norm_mod_matmul_glu(x, g, scale, shift, w, *, tm, tn, out_dtype):
    m, n = x.shape[0], w.shape[1] // 2
    nj = n // tn
    return pl.pallas_call(
        _nmm_glu_kernel,
        grid=(m // tm, nj),
        in_specs=_nmm_common_specs(tm) + [
            pl.BlockSpec((D_MODEL, tn), lambda i, j: (0, j)),
            pl.BlockSpec((D_MODEL, tn), lambda i, j: (0, j + nj)),
        ],
        out_specs=pl.BlockSpec((tm, tn), lambda i, j: (i, j)),
        out_shape=jax.ShapeDtypeStruct((m, n), out_dtype),
        scratch_shapes=[pltpu.VMEM((tm, D_MODEL), BF16)],
        compiler_params=_params(2),
        name="norm_mod_matmul_glu",
    )(x, g, scale, shift, w, w)


def norm_mod_matmul_dil(x, g, scale, shift, w, nw, qs, cos, sin, *, group, dil, tm):
    m = x.shape[0]
    tn = DIL_COLS
    return pl.pallas_call(
        functools.partial(_nmm_dil_kernel, dil=dil),
        grid=(m // tm, 3),
        in_specs=_nmm_common_specs(tm) + [
            pl.BlockSpec((D_MODEL, tn), lambda i, j: (0, 3 * group + j)),
            pl.BlockSpec((None, 1, DIL_HEAD_DIM), lambda i, j: (3 * group + j, 0, 0)),
            pl.BlockSpec((None, 1, DIL_HEAD_DIM), lambda i, j: (3 * group + j, 0, 0)),
            pl.BlockSpec((tm, DIL_HEAD_DIM), lambda i, j: (i, 0)),
            pl.BlockSpec((tm, DIL_HEAD_DIM), lambda i, j: (i, 0)),
        ],
        out_specs=pl.BlockSpec((tm // dil, dil * tn), lambda i, j: (i, j)),
        out_shape=jax.ShapeDtypeStruct((m // dil, 3 * dil * tn), BF16),
        scratch_shapes=[pltpu.VMEM((tm, D_MODEL), BF16),
                        pltpu.VMEM((DIL_HEADS, tm, DIL_HEAD_DIM), F32)],
        compiler_params=_params(2),
        name=f"norm_mod_matmul_dil_g{group}",
    )(x, g, scale, shift, w, nw, qs, cos, sin)


def _mmres_kernel(a_ref, w_ref, x_ref, gate_ref, o_ref):
    y = _bdot(a_ref[...], w_ref[...].astype(BF16))
    o_ref[...] = x_ref[...] + (1.0 + gate_ref[...]) * y


def matmul_residual(a, w, x, gate, *, tm, tn):
    m, k = a.shape
    n = w.shape[1]
    return pl.pallas_call(
        _mmres_kernel,
        grid=(m // tm, n // tn),
        in_specs=[
            pl.BlockSpec((tm, k), lambda i, j: (i, 0)),
            pl.BlockSpec((k, tn), lambda i, j: (0, j)),
            pl.BlockSpec((tm, tn), lambda i, j: (i, j)),
            pl.BlockSpec((1, tn), lambda i, j: (0, j)),
        ],
        out_specs=pl.BlockSpec((tm, tn), lambda i, j: (i, j)),
        out_shape=jax.ShapeDtypeStruct((m, n), F32),
        compiler_params=_params(2),
        name="matmul_residual",
    )(a, w, x, gate)


def _conv_kernel(prev_ref, cur_ref, next_ref, dw_ref, lg_ref, lb_ref, o_ref, ext_ref, cv_ref, zs_ref,
                 *, tr):
    i = pl.program_id(0)
    last = pl.num_programs(0) - 1
    ext_ref[0:CONV_HALO, :] = jnp.where(i > 0, prev_ref[...], 0.0)
    ext_ref[CONV_HALO:CONV_HALO + tr, :] = cur_ref[...]
    ext_ref[CONV_HALO + tr:, :] = jnp.where(i < last, next_ref[...], 0.0)
    base = CONV_HALO - CONV_WIDTH // 2
    n_chunks = D_MODEL // 128

    def conv_chunk(c, carry):
        cs = pl.ds(pl.multiple_of(c * 128, 128), 128)
        for r0 in range(0, tr, CONV_ROWS):
            acc = None
            for b in range(8):
                z = None
                for k in range(CONV_WIDTH):
                    if (base + k) % 8 != b:
                        continue
                    off = r0 + base + k - b
                    term = ext_ref[off:off + CONV_ROWS + 8, cs] * dw_ref[k:k + 1, cs]
                    z = term if z is None else z + term
                if b == 0:
                    part = z[0:CONV_ROWS]
                else:
                    slot = (r0 // CONV_ROWS) * 8 + b
                    zs_ref[slot] = z
                    part = zs_ref[slot, b:b + CONV_ROWS, :]
                acc = part if acc is None else acc + part
            cv_ref[r0:r0 + CONV_ROWS, cs] = acc
        return carry

    lax.fori_loop(0, n_chunks, conv_chunk, 0)

    def sum_chunk(c, s1):
        cs = pl.ds(pl.multiple_of(c * 128, 128), 128)
        return s1 + jnp.sum(cv_ref[:, cs], axis=1, keepdims=True)

    s1 = lax.fori_loop(0, n_chunks, sum_chunk, jnp.zeros((tr, 1), F32))
    mu = s1 * (1.0 / D_MODEL)

    def var_chunk(c, s2):
        cs = pl.ds(pl.multiple_of(c * 128, 128), 128)
        dlt = cv_ref[:, cs] - mu
        return s2 + jnp.sum(dlt * dlt, axis=1, keepdims=True)

    s2 = lax.fori_loop(0, n_chunks, var_chunk, jnp.zeros((tr, 1), F32))
    rstd = lax.rsqrt(s2 * (1.0 / D_MODEL) + EPS)

    def out_chunk(c, carry):
        cs = pl.ds(pl.multiple_of(c * 128, 128), 128)
        un = (cv_ref[:, cs] - mu) * rstd * lg_ref[:, cs] + lb_ref[:, cs]
        o_ref[:, cs] = (un * jax.nn.sigmoid(un)).astype(o_ref.dtype)
        return carry

    lax.fori_loop(0, n_chunks, out_chunk, 0)


def conv_ln_swish(u, dw, ln_g, ln_b, *, tr):
    m = u.shape[0]
    hb = tr // CONV_HALO
    n_hb = m // CONV_HALO
    return pl.pallas_call(
        functools.partial(_conv_kernel, tr=tr),
        grid=(m // tr,),
        in_specs=[
            pl.BlockSpec((CONV_HALO, D_MODEL), lambda i: (jnp.maximum(i * hb - 1, 0), 0)),
            pl.BlockSpec((tr, D_MODEL), lambda i: (i, 0)),
            pl.BlockSpec((CONV_HALO, D_MODEL), lambda i: (jnp.minimum((i + 1) * hb, n_hb - 1), 0)),
            pl.BlockSpec((CONV_WIDTH, D_MODEL), lambda i: (0, 0)),
            pl.BlockSpec((1, D_MODEL), lambda i: (0, 0)),
            pl.BlockSpec((1, D_MODEL), lambda i: (0, 0)),
        ],
        out_specs=pl.BlockSpec((tr, D_MODEL), lambda i: (i, 0)),
        out_shape=jax.ShapeDtypeStruct((m, D_MODEL), BF16),
        scratch_shapes=[
            pltpu.VMEM((tr + 2 * CONV_HALO, D_MODEL), F32),
            pltpu.VMEM((tr, D_MODEL), F32),
            pltpu.VMEM((8 * (tr // CONV_ROWS), CONV_ROWS + 8, 128), F32),
        ],
        compiler_params=_params(1),
        name="conv_ln_swish",
    )(u, u, u, dw, ln_g, ln_b)


def _pool_kernel(prev_ref, cur_ref, next_ref, wg_ref, cs_ref, o_ref, ext_ref, mix_ref, *, tr):
    i = pl.program_id(0)
    last = pl.num_programs(0) - 1
    ext_ref[0:POOL_HALO, :] = jnp.where(i > 0, prev_ref[...], 0.0)
    ext_ref[POOL_HALO:POOL_HALO + tr, :] = cur_ref[...]
    ext_ref[POOL_HALO + tr:, :] = jnp.where(i < last, next_ref[...], 0.0)
    t = i * tr + lax.broadcasted_iota(jnp.int32, (tr, 1), 0)
    for g, window in enumerate(POOL_WINDOWS):
        r = window // 2
        cnt = (jnp.minimum(t + r + 1, SEQ) - jnp.maximum(t - r, 0)).astype(F32)
        inv_cnt = 1.0 / cnt
        for c in range(POOL_GROUP_DIM // 128):
            lo = g * POOL_GROUP_DIM + c * 128
            acc = jnp.zeros((tr, 128), F32)
            for dlt in range(-r, r + 1):
                acc = acc + ext_ref[POOL_HALO + dlt:POOL_HALO + dlt + tr, lo:lo + 128]
            mix = acc * inv_cnt - ext_ref[POOL_HALO:POOL_HALO + tr, lo:lo + 128]
            mix_ref[:, lo:lo + 128] = mix.astype(BF16)
    for g in range(len(POOL_WINDOWS)):
        cols = slice(g * POOL_GROUP_DIM, (g + 1) * POOL_GROUP_DIM)
        y = _bdot(mix_ref[:, cols], wg_ref[g].astype(BF16))
        o_ref[:, cols] = (y * cs_ref[:, cols]).astype(o_ref.dtype)


def pool_mix(u, w_grp, ch_scale, *, tr):
    m = u.shape[0]
    hb = tr // POOL_HALO
    n_hb = m // POOL_HALO
    n_grp = len(POOL_WINDOWS)
    return pl.pallas_call(
        functools.partial(_pool_kernel, tr=tr),
        grid=(m // tr,),
        in_specs=[
            pl.BlockSpec((POOL_HALO, D_MODEL), lambda i: (jnp.maximum(i * hb - 1, 0), 0)),
            pl.BlockSpec((tr, D_MODEL), lambda i: (i, 0)),
            pl.BlockSpec((POOL_HALO, D_MODEL), lambda i: (jnp.minimum((i + 1) * hb, n_hb - 1), 0)),
            pl.BlockSpec((n_grp, POOL_GROUP_DIM, POOL_GROUP_DIM), lambda i: (0, 0, 0)),
            pl.BlockSpec((1, D_MODEL), lambda i: (0, 0)),
        ],
        out_specs=pl.BlockSpec((tr, D_MODEL), lambda i: (i, 0)),
        out_shape=jax.ShapeDtypeStruct((m, D_MODEL), BF16),
        scratch_shapes=[
            pltpu.VMEM((tr + 2 * POOL_HALO, D_MODEL), F32),
            pltpu.VMEM((tr, D_MODEL), BF16),
        ],
        compiler_params=_params(1),
        name="pool_mix",
    )(u, u, u, w_grp, ch_scale)


def _dil_attn_kernel(q_ref, kp_ref, kc_ref, kn_ref, vp_ref, vc_ref, vn_ref,
                     o_ref, m_ref, l_ref, *, tq, n_slots):
    i = pl.program_id(1)
    row = lax.broadcasted_iota(jnp.int32, (tq, 3 * tq), 0)
    col = lax.broadcasted_iota(jnp.int32, (tq, 3 * tq), 1)
    key_slot = i * tq - tq + col
    valid = (jnp.abs(col - tq - row) <= DIL_HALF) & (key_slot >= 0) & (key_slot < n_slots)
    for h in range(DIL_HEADS):
        hs = slice(h * DIL_HEAD_DIM, (h + 1) * DIL_HEAD_DIM)
        q = q_ref[:, hs]
        s = jnp.concatenate(
            [_bdot_t(q, kp_ref[:, hs]), _bdot_t(q, kc_ref[:, hs]), _bdot_t(q, kn_ref[:, hs])], axis=1)
        s = jnp.where(valid, s, -jnp.inf)
        m = jnp.max(s, axis=1, keepdims=True)
        p = jnp.exp(s - m)
        l = jnp.sum(p, axis=1, keepdims=True)
        pb = p.astype(BF16)
        o = (_bdot(pb[:, 0:tq], vp_ref[:, hs]) + _bdot(pb[:, tq:2 * tq], vc_ref[:, hs])
             + _bdot(pb[:, 2 * tq:], vn_ref[:, hs]))
        o_ref[:, hs] = o
        m_ref[:, hs] = jnp.broadcast_to(m, (tq, DIL_HEAD_DIM))
        l_ref[:, hs] = jnp.broadcast_to(l, (tq, DIL_HEAD_DIM))


def dil_group_attention(qkv, group, dil, *, tq):
    n_slots = SEQ // dil
    nb = n_slots // tq

    def spec(slab, shift):
        return pl.BlockSpec(
            (tq, DIL_COLS), lambda p, i: (jnp.clip(i + shift, 0, nb - 1), slab * dil + p))

    out_spec = pl.BlockSpec((tq, DIL_COLS), lambda p, i: (i, p))
    out_sds = jax.ShapeDtypeStruct((n_slots, dil * DIL_COLS), F32)
    return pl.pallas_call(
        functools.partial(_dil_attn_kernel, tq=tq, n_slots=n_slots),
        grid=(dil, nb),
        in_specs=[spec(0, 0),
                  spec(1, -1), spec(1, 0), spec(1, 1),
                  spec(2, -1), spec(2, 0), spec(2, 1)],
        out_specs=[out_spec, out_spec, out_spec],
        out_shape=[out_sds, out_sds, out_sds],
        compiler_params=_params(2),
        name=f"dil_attn_g{group}",
    )(qkv, qkv, qkv, qkv, qkv, qkv, qkv)


def _dil_merge_kernel(*refs, dils):
    n_in = 3 * len(dils)
    in_refs, out_ref, scratch = refs[:n_in], refs[n_in], list(refs[n_in + 1:])
    tm = out_ref.shape[0]
    for h in range(DIL_HEADS):
        tok = []
        for gi, dil in enumerate(dils):
            for a, src in enumerate(in_refs[3 * gi:3 * gi + 3]):
                if dil == 1:
                    tok.append(src[:, h * DIL_HEAD_DIM:(h + 1) * DIL_HEAD_DIM])
                    continue
                dst = scratch.pop(0)
                for p in range(dil):
                    lo = p * DIL_COLS + h * DIL_HEAD_DIM
                    dst[h, pl.ds(p, tm // dil, stride=dil), :] = src[:, lo:lo + DIL_HEAD_DIM]
                tok.append(dst[h])
                scratch.append(dst)
        o0, m0, l0, o1, m1, l1, o2, m2, l2 = tok
        mm = jnp.maximum(jnp.maximum(m0, m1), m2)
        w0 = jnp.exp(m0 - mm)
        w1 = jnp.exp(m1 - mm)
        w2 = jnp.exp(m2 - mm)
        num = w0 * o0 + w1 * o1 + w2 * o2
        den = w0 * l0 + w1 * l1 + w2 * l2
        out_ref[:, h * DIL_HEAD_DIM:(h + 1) * DIL_HEAD_DIM] = (num / den).astype(out_ref.dtype)


def dil_merge(parts, dils, *, tm):
    in_specs = []
    n_scratch = 0
    for dil in dils:
        in_specs += [pl.BlockSpec((tm // dil, dil * DIL_COLS), lambda i: (i, 0))] * 3
        n_scratch += 3 if dil > 1 else 0
    flat = [a for part in parts for a in part]
    return pl.pallas_call(
        functools.partial(_dil_merge_kernel, dils=tuple(dils)),
        grid=(SEQ // tm,),
        in_specs=in_specs,
        out_specs=pl.BlockSpec((tm, DIL_COLS), lambda i: (i, 0)),
        out_shape=jax.ShapeDtypeStruct((SEQ, DIL_COLS), BF16),
        scratch_shapes=[pltpu.VMEM((DIL_HEADS, tm, DIL_HEAD_DIM), F32)] * n_scratch,
        compiler_params=_params(1),
        name="dil_merge",
    )(*flat)


def _mla_prep_kernel(u_ref, qan_ref, kvan_ref, wq_ref, wkv_ref, qn_ref, kn_ref,
                     cos_ref, sa_ref, sb_ref, q_out, k_out, vt_out, cq_s, ckv_s):
    @pl.when(pl.program_id(1) == 0)
    def _():
        cq = u_ref[:, 0:MLA_Q_RANK]
        cq_s[...] = (cq * lax.rsqrt(jnp.mean(cq * cq, axis=-1, keepdims=True) + EPS)
                     * qan_ref[...]).astype(BF16)
        ckv = u_ref[:, MLA_Q_RANK:MLA_Q_RANK + MLA_KV_RANK]
        ckv_s[...] = (ckv * lax.rsqrt(jnp.mean(ckv * ckv, axis=-1, keepdims=True) + EPS)
                      * kvan_ref[...]).astype(BF16)

    cos = cos_ref[...]
    sa = sa_ref[...]
    sb = sb_ref[...]

    def norm_rope(nope, rp, w_ref):
        ssq = jnp.sum(nope * nope, axis=-1, keepdims=True) + jnp.sum(rp * rp, axis=-1, keepdims=True)
        r = lax.rsqrt(ssq * (1.0 / MLA_QK) + EPS)
        nope_n = nope * r * w_ref[:, 0:128]
        rp_n = rp * r * w_ref[:, 128:256]
        rp_r = rp_n * cos + pltpu.roll(rp_n, 96, 1) * sa + pltpu.roll(rp_n, 32, 1) * sb
        return nope_n, rp_r

    q = _bdot(cq_s[...], wq_ref[...].astype(BF16))
    kv = _bdot(ckv_s[...], wkv_ref[...].astype(BF16))
    kr = u_ref[:, MLA_Q_RANK + MLA_KV_RANK:MLA_Q_RANK + MLA_KV_RANK + 128]
    scale = MLA_QK ** -0.5 * LOG2_E
    qn, qr = norm_rope(q[:, 0:128], q[:, 128:256], qn_ref)
    q_out[:, 0:128] = (qn * scale).astype(BF16)
    q_out[:, 128:256] = (qr * scale).astype(BF16)
    kn, kr_r = norm_rope(kv[:, 0:128], kr, kn_ref)
    k_out[:, 0:128] = kn.astype(BF16)
    k_out[:, 128:256] = kr_r.astype(BF16)
    vt_out[...] = kv[:, 128:256].T.astype(BF16)


def mla_prep(u, q_a_norm, kv_a_norm, wq_pad, w_kv_up, qn_pad, kn_pad, cos, sa, sb, *, tm):
    m = u.shape[0]
    ucols = u.shape[1]
    vec = lambda n: pl.BlockSpec((1, n), lambda i, h: (0, 0))
    tab = pl.BlockSpec((tm, 128), lambda i, h: (i, 0))
    head_out = lambda w: pl.BlockSpec((None, tm, w), lambda i, h: (h, i, 0))
    return pl.pallas_call(
        _mla_prep_kernel,
        grid=(m // tm, MLA_HEADS),
        in_specs=[
            pl.BlockSpec((tm, ucols), lambda i, h: (i, 0)),
            vec(MLA_Q_RANK), vec(MLA_KV_RANK),
            pl.BlockSpec((MLA_Q_RANK, MLA_PAD), lambda i, h: (0, h)),
            pl.BlockSpec((MLA_KV_RANK, MLA_NOPE + MLA_V), lambda i, h: (0, h)),
            vec(MLA_PAD), vec(MLA_PAD),
            tab, tab, tab,
        ],
        out_specs=[head_out(MLA_PAD), head_out(MLA_PAD),
                   pl.BlockSpec((None, MLA_V, tm), lambda i, h: (h, 0, i))],
        out_shape=[
            jax.ShapeDtypeStruct((MLA_HEADS, m, MLA_PAD), BF16),
            jax.ShapeDtypeStruct((MLA_HEADS, m, MLA_PAD), BF16),
            jax.ShapeDtypeStruct((MLA_HEADS, MLA_V, m), BF16),
        ],
        scratch_shapes=[pltpu.VMEM((tm, MLA_Q_RANK), BF16), pltpu.VMEM((tm, MLA_KV_RANK), BF16)],
        compiler_params=_params(2),
        name="mla_prep",
    )(u, q_a_norm, kv_a_norm, wq_pad, w_kv_up, qn_pad, kn_pad, cos, sa, sb)


def _mla_attn_kernel(q_ref, k_ref, vt_ref, o_ref, m_s, l_s, acc_s, sa_ref, sb_ref, *, tk):
    q = q_ref[...]
    n_chunks = k_ref.shape[0] // tk
    assert n_chunks % 2 == 0
    m_s[...] = jnp.full(m_s.shape, -jnp.inf, F32)
    l_s[...] = jnp.zeros(l_s.shape, F32)
    acc_s[...] = jnp.zeros(acc_s.shape, F32)

    def key_rows(c):
        return pl.ds(pl.multiple_of(c * tk, tk), tk)

    def scores(c, dst):
        dst[...] = _bdot_t(k_ref[key_rows(c), :], q)

    def consume(c, src):
        st = src[...]
        m_prev = m_s[...]
        m_new = jnp.maximum(m_prev, jnp.max(st, axis=0, keepdims=True))
        alpha = jnp.exp2(m_prev - m_new)
        pt = jnp.exp2(st - m_new)
        l_s[...] = alpha * l_s[...] + jnp.sum(pt, axis=0, keepdims=True)
        acc_s[...] = alpha * acc_s[...] + _bdot(vt_ref[:, key_rows(c)], pt.astype(BF16))
        m_s[...] = m_new

    scores(0, sa_ref)

    def body(c2, carry):
        c = 2 * c2
        scores(c + 1, sb_ref)
        consume(c, sa_ref)
        scores(c + 2, sa_ref)
        consume(c + 1, sb_ref)
        return carry

    lax.fori_loop(0, n_chunks // 2 - 1, body, 0)
    scores(n_chunks - 1, sb_ref)
    consume(n_chunks - 2, sa_ref)
    consume(n_chunks - 1, sb_ref)
    o_ref[...] = (acc_s[...] / l_s[...]).T.astype(o_ref.dtype)


def mla_attention(q, k, vt, *, tq, tk):
    n_heads, m, _ = q.shape
    return pl.pallas_call(
        functools.partial(_mla_attn_kernel, tk=tk),
        grid=(n_heads, m // tq),
        in_specs=[
            pl.BlockSpec((None, tq, MLA_PAD), lambda h, i: (h, i, 0)),
            pl.BlockSpec((None, m, MLA_PAD), lambda h, i: (h, 0, 0)),
            pl.BlockSpec((None, MLA_V, m), lambda h, i: (h, 0, 0)),
        ],
        out_specs=pl.BlockSpec((tq, MLA_V), lambda h, i: (i, h)),
        out_shape=jax.ShapeDtypeStruct((m, n_heads * MLA_V), BF16),
        scratch_shapes=[
            pltpu.VMEM((1, tq), F32), pltpu.VMEM((1, tq), F32), pltpu.VMEM((MLA_V, tq), F32),
            pltpu.VMEM((tk, tq), F32), pltpu.VMEM((tk, tq), F32)],
        compiler_params=_params(2),
        name="mla_attention",
    )(q, k, vt)


def _router_kernel(x_ref, g_ref, sc_ref, sh_ref, wr_ref, hn_out, aff_out):
    hn = _norm_mod(x_ref[...], g_ref[...], sc_ref[...], sh_ref[...])
    hn_out[...] = hn.astype(BF16)
    logits = jnp.dot(hn, wr_ref[...], preferred_element_type=F32, precision=lax.Precision.HIGHEST)
    z = jnp.exp(logits - jnp.max(logits, axis=-1, keepdims=True))
    aff_out[...] = z / jnp.sum(z, axis=-1, keepdims=True)


def moe_router(x, g, scale, shift, w_router, *, tm):
    m = x.shape[0]
    vec = pl.BlockSpec((1, D_MODEL), lambda i: (0, 0))
    return pl.pallas_call(
        _router_kernel,
        grid=(m // tm,),
        in_specs=[pl.BlockSpec((tm, D_MODEL), lambda i: (i, 0)), vec, vec, vec,
                  pl.BlockSpec((D_MODEL, N_EXPERTS), lambda i: (0, 0))],
        out_specs=[pl.BlockSpec((tm, D_MODEL), lambda i: (i, 0)),
                   pl.BlockSpec((tm, N_EXPERTS), lambda i: (i, 0))],
        out_shape=[jax.ShapeDtypeStruct((m, D_MODEL), BF16),
                   jax.ShapeDtypeStruct((m, N_EXPERTS), F32)],
        compiler_params=_params(1),
        name="moe_router",
    )(x, g, scale, shift, w_router)


FFN_OUT_CHUNK = 512


def _ffn_kernel(xs_ref, wg_ref, wu_ref, wd_ref, gt_ref, rg_ref, o_ref):
    f = pl.program_id(1)
    xs = xs_ref[...]
    a = _bdot(xs, wg_ref[...].astype(BF16))
    b = _bdot(xs, wu_ref[...].astype(BF16))
    hmid = (a * jax.nn.sigmoid(a) * b).astype(BF16)
    last = f == pl.num_programs(1) - 1
    for n in range(D_MODEL // FFN_OUT_CHUNK):
        cols = slice(n * FFN_OUT_CHUNK, (n + 1) * FFN_OUT_CHUNK)
        y = _bdot(hmid, wd_ref[:, cols].astype(BF16))

        @pl.when(f == 0)
        def _(y=y, cols=cols):
            o_ref[:, cols] = y

        @pl.when(jnp.logical_and(f > 0, jnp.logical_not(last)))
        def _(y=y, cols=cols):
            o_ref[:, cols] += y

        @pl.when(last)
        def _(y=y, cols=cols):
            cs = slice(cols.start, cols.stop)
            o_ref[:, cs] = (o_ref[:, cs] + y) * gt_ref[...] * (1.0 + rg_ref[:, cs])


def expert_ffn(xs, w_gate, w_up, w_down, gates, res_gate, *, layer, tf):
    n_exp, cap, _ = xs.shape
    return pl.pallas_call(
        _ffn_kernel,
        grid=(n_exp, EXPERT_FF // tf),
        in_specs=[
            pl.BlockSpec((None, cap, D_MODEL), lambda e, f: (e, 0, 0)),
            pl.BlockSpec((None, None, D_MODEL, tf), lambda e, f: (layer, e, 0, f)),
            pl.BlockSpec((None, None, D_MODEL, tf), lambda e, f: (layer, e, 0, f)),
            pl.BlockSpec((None, None, tf, D_MODEL), lambda e, f: (layer, e, f, 0)),
            pl.BlockSpec((None, cap, 1), lambda e, f: (e, 0, 0)),
            pl.BlockSpec((1, D_MODEL), lambda e, f: (0, 0)),
        ],
        out_specs=pl.BlockSpec((None, cap, D_MODEL), lambda e, f: (e, 0, 0)),
        out_shape=jax.ShapeDtypeStruct((n_exp, cap, D_MODEL), F32),
        compiler_params=_params(2),
        name="expert_ffn",
    )(xs, w_gate, w_up, w_down, gates, res_gate)


def moe_layer(x, g, scale, shift, gate, w_router, w_gate, w_up, w_down, *, layer):
    hn, aff = moe_router(x, g, scale, shift, w_router, tm=512)
    gates, idx = lax.top_k(aff.T, EXPERT_CAP)
    xs = jnp.take(hn, idx.reshape(-1), axis=0).reshape(N_EXPERTS, EXPERT_CAP, D_MODEL)
    y = expert_ffn(xs, w_gate, w_up, w_down, gates[..., None], gate, layer=layer, tf=256)
    return x.at[idx.reshape(-1)].add(y.reshape(-1, D_MODEL))


def _rope_tables(positions, dim):
    half = dim // 2
    inv = ROPE_THETA ** (-jnp.arange(half, dtype=F32) / half)
    ang = positions.astype(F32)[:, None] * inv
    return jnp.cos(ang), jnp.sin(ang)


def _row(v):
    return v.reshape(1, -1)


def conv_mixer(xs, g, shift, scale, gate, w_in, dw, ln_g, ln_b, w_out):
    u = norm_mod_matmul_glu(xs, g, scale, shift, w_in, tm=1024, tn=512, out_dtype=F32)
    v = conv_ln_swish(u, dw, _row(ln_g), _row(ln_b), tr=256)
    return matmul_residual(v, w_out, xs, gate, tm=1024, tn=512)


def pool_mixer(xs, g, shift, scale, gate, w_in, w_grp, ch_scale, w_out):
    u = norm_mod_matmul(xs, g, scale, shift, w_in, tm=1024, tn=512, out_dtype=F32)
    v = pool_mix(u, w_grp, _row(ch_scale), tr=256)
    return matmul_residual(v, w_out, xs, gate, tm=1024, tn=512)


def dil_mixer(xs, g, shift, scale, gate, pos, w_in, q_norm, k_norm, w_out):
    cos, sin = _rope_tables(pos, DIL_HEAD_DIM)
    cos_f = jnp.concatenate([cos, cos], axis=1)
    sin_f = jnp.concatenate([-sin, sin], axis=1)
    ones = jnp.ones((DIL_HEAD_DIM,), F32)
    nw = jnp.stack([w for grp in range(len(DIL_PATTERNS))
                    for w in (q_norm[grp], k_norm[grp], ones)])[:, None, :]
    qscale = jnp.asarray([DIL_HEAD_DIM ** -0.5, 1.0, 1.0] * len(DIL_PATTERNS), F32)
    qs = jnp.broadcast_to(qscale[:, None, None], nw.shape)
    dils = [dil for _, dil in DIL_PATTERNS]
    parts = []
    for grp, dil in enumerate(dils):
        qkv = norm_mod_matmul_dil(xs, g, scale, shift, w_in, nw, qs, cos_f, sin_f,
                                  group=grp, dil=dil, tm=1024)
        parts.append(dil_group_attention(qkv, grp, dil, tq=256))
    o = dil_merge(parts, dils, tm=256)
    return matmul_residual(o, w_out, xs, gate, tm=1024, tn=512)


def mla_mixer(xs, g, shift, scale, gate, pos, w_in, q_a_norm, w_q_up, kv_a_norm, w_kv_up,
              q_norm, k_norm, w_out):
    cos, sin = _rope_tables(pos, MLA_ROPE)
    z32 = jnp.zeros_like(sin)
    z64 = jnp.zeros((SEQ, 64), F32)
    cos_p = jnp.concatenate([cos, cos, z64], axis=1)
    sin_a = jnp.concatenate([-sin, z32, z64], axis=1)
    sin_b = jnp.concatenate([z32, sin, z64], axis=1)
    w_in_pad = jnp.pad(w_in, ((0, 0), (0, 64)))
    wq_pad = jnp.pad(w_q_up.reshape(MLA_Q_RANK, MLA_HEADS, MLA_QK),
                     ((0, 0), (0, 0), (0, MLA_PAD - MLA_QK))).reshape(MLA_Q_RANK, MLA_HEADS * MLA_PAD)
    qn_pad = jnp.pad(q_norm, (0, MLA_PAD - MLA_QK)).reshape(1, MLA_PAD)
    kn_pad = jnp.pad(k_norm, (0, MLA_PAD - MLA_QK)).reshape(1, MLA_PAD)
    u = norm_mod_matmul(xs, g, scale, shift, w_in_pad, tm=512, tn=w_in_pad.shape[1], out_dtype=F32)
    q, k, vt = mla_prep(u, _row(q_a_norm), _row(kv_a_norm), wq_pad, w_kv_up, qn_pad, kn_pad,
                       cos_p, sin_a, sin_b, tm=512)
    o = mla_attention(q, k, vt, tq=1024, tk=1024)
    return matmul_residual(o, w_out, xs, gate, tm=1024, tn=512)


def kernel(x, c, positions, norm_g, ada_w, ada_b, conv_w_in, conv_dw, conv_ln_g, conv_ln_b, conv_w_out, pool_w_in, pool_w_grp, pool_scale, pool_w_out, dil_w_in, dil_q_norm, dil_k_norm, dil_w_out, mla_w_in, mla_q_a_norm, mla_w_q_up, mla_kv_a_norm, mla_w_kv_up, mla_q_norm, mla_k_norm, mla_w_out, moe_router, moe_w_gate, moe_w_up, moe_w_down):
    assert x.shape == (1, SEQ, D_MODEL) and ada_w.shape[0] == DEPTH
    xs = x.reshape(SEQ, D_MODEL)
    pos = positions.reshape(SEQ)
    mod = ada_all(c, ada_w, ada_b)

    def modulation(layer, sub):
        mrow = mod[2 * layer + sub]
        return mrow[:, 0:D_MODEL], mrow[:, D_MODEL:2 * D_MODEL], mrow[:, 2 * D_MODEL:]

    for layer in range(DEPTH):
        mixer = layer % 4
        occ = layer // 4
        shift, scale, gate = modulation(layer, 0)
        g = _row(norm_g[layer, 0])
        if mixer == 0:
            xs = conv_mixer(xs, g, shift, scale, gate, conv_w_in[occ], conv_dw[occ], conv_ln_g[occ],
                            conv_ln_b[occ], conv_w_out[occ])
        elif mixer == 1:
            xs = pool_mixer(xs, g, shift, scale, gate, pool_w_in[occ], pool_w_grp[occ], pool_scale[occ],
                            pool_w_out[occ])
        elif mixer == 2:
            xs = dil_mixer(xs, g, shift, scale, gate, pos, dil_w_in[occ], dil_q_norm[occ], dil_k_norm[occ],
                           dil_w_out[occ])
        else:
            xs = mla_mixer(xs, g, shift, scale, gate, pos, mla_w_in[occ], mla_q_a_norm[occ], mla_w_q_up[occ],
                           mla_kv_a_norm[occ], mla_w_kv_up[occ], mla_q_norm[occ], mla_k_norm[occ],
                           mla_w_out[occ])
        shift, scale, gate = modulation(layer, 1)
        xs = moe_layer(xs, _row(norm_g[layer, 1]), scale, shift, gate, moe_router[layer],
                       moe_w_gate, moe_w_up, moe_w_down, layer=layer)
    return xs.reshape(1, SEQ, D_MODEL)
```

```python
import functools

import jax
import jax.numpy as jnp
from jax import lax
from jax.experimental import pallas as pl
from jax.experimental.pallas import tpu as pltpu

F32 = jnp.float32
BF16 = jnp.bfloat16

D_MODEL = 2048
SEQ = 8192
DEPTH = 4
EPS = 1e-6
ROPE_THETA = 10000.0
LOG2_E = 1.4426950408889634

CONV_WIDTH = 31
CONV_HALO = 16
CONV_ROWS = 128
POOL_WINDOWS = (2, 4, 8, 16)
POOL_GROUP_DIM = D_MODEL // len(POOL_WINDOWS)
POOL_HALO = 8

DIL_PATTERNS = ((128, 1), (512, 4), (2048, 16))
DIL_HEADS = 8
DIL_HEAD_DIM = 128
DIL_COLS = DIL_HEADS * DIL_HEAD_DIM
DIL_HALF = 64

MLA_HEADS = 16
MLA_Q_RANK = 512
MLA_KV_RANK = 512
MLA_NOPE = 128
MLA_ROPE = 64
MLA_V = 128
MLA_QK = MLA_NOPE + MLA_ROPE
MLA_PAD = 256

N_EXPERTS = 16
EXPERT_FF = 1024
EXPERT_CAP = 2 * SEQ // N_EXPERTS

VMEM_LIMIT_BYTES = 56 * 1024 * 1024


def _params(n_axes):
    return pltpu.CompilerParams(
        dimension_semantics=("arbitrary",) * n_axes, vmem_limit_bytes=VMEM_LIMIT_BYTES)


def _bdot(a, b):
    return jnp.dot(a, b, preferred_element_type=F32)


def _bdot_t(a, b):
    return lax.dot_general(a, b, (((1,), (1,)), ((), ())), preferred_element_type=F32)


def _ada_kernel(c_ref, w_ref, b_ref, o_ref):
    c = c_ref[...]
    cond = c * jax.nn.sigmoid(c)
    o_ref[...] = _bdot(cond.astype(BF16), w_ref[...].astype(BF16)) + b_ref[...]


def ada_all(c, ada_w, ada_b):
    n = ada_w.shape[0] * ada_w.shape[1]
    w = ada_w.reshape(n, D_MODEL, 3 * D_MODEL)
    b = ada_b.reshape(n, 1, 3 * D_MODEL)
    c8 = jnp.broadcast_to(c.reshape(1, D_MODEL), (8, D_MODEL))
    tn = 1024
    out = pl.pallas_call(
        _ada_kernel,
        grid=(n, 3 * D_MODEL // tn),
        in_specs=[
            pl.BlockSpec((8, D_MODEL), lambda l, j: (0, 0)),
            pl.BlockSpec((None, D_MODEL, tn), lambda l, j: (l, 0, j)),
            pl.BlockSpec((None, 1, tn), lambda l, j: (l, 0, j)),
        ],
        out_specs=pl.BlockSpec((None, 8, tn), lambda l, j: (l, 0, j)),
        out_shape=jax.ShapeDtypeStruct((n, 8, 3 * D_MODEL), F32),
        compiler_params=_params(2),
        name="ada_all",
    )(c8, w, b)
    return out[:, 0:1, :]


def _norm_mod(x, g, scale, shift):
    ms = jnp.mean(x * x, axis=-1, keepdims=True)
    y = x * lax.rsqrt(ms + EPS) * g
    return y * (1.0 + scale) + shift


def _nmm_plain_kernel(x_ref, g_ref, sc_ref, sh_ref, w_ref, o_ref, hn_ref):
    @pl.when(pl.program_id(1) == 0)
    def _():
        hn_ref[...] = _norm_mod(x_ref[...], g_ref[...], sc_ref[...], sh_ref[...]).astype(BF16)

    o_ref[...] = _bdot(hn_ref[...], w_ref[...].astype(BF16)).astype(o_ref.dtype)


def _nmm_glu_kernel(x_ref, g_ref, sc_ref, sh_ref, wa_ref, wb_ref, o_ref, hn_ref):
    @pl.when(pl.program_id(1) == 0)
    def _():
        hn_ref[...] = _norm_mod(x_ref[...], g_ref[...], sc_ref[...], sh_ref[...]).astype(BF16)

    hn = hn_ref[...]
    a = _bdot(hn, wa_ref[...].astype(BF16))
    b = _bdot(hn, wb_ref[...].astype(BF16))
    o_ref[...] = (a * jax.nn.sigmoid(b)).astype(o_ref.dtype)


def _nmm_dil_kernel(x_ref, g_ref, sc_ref, sh_ref, w_ref, nw_ref, qs_ref, cos_ref, sin_ref,
                    o_ref, hn_ref, stage_ref, *, dil):
    j = pl.program_id(1)

    @pl.when(j == 0)
    def _():
        hn_ref[...] = _norm_mod(x_ref[...], g_ref[...], sc_ref[...], sh_ref[...]).astype(BF16)

    acc = _bdot(hn_ref[...], w_ref[...].astype(BF16))
    is_v = j == 2

    @pl.when(is_v)
    def _():
        for h in range(DIL_HEADS):
            stage_ref[h] = acc[:, h * DIL_HEAD_DIM:(h + 1) * DIL_HEAD_DIM]

    @pl.when(jnp.logical_not(is_v))
    def _():
        nw = nw_ref[...] * qs_ref[...]
        cos = cos_ref[...]
        sin = sin_ref[...]
        for h in range(DIL_HEADS):
            a = acc[:, h * DIL_HEAD_DIM:(h + 1) * DIL_HEAD_DIM]
            ms = jnp.mean(a * a, axis=-1, keepdims=True)
            an = a * lax.rsqrt(ms + EPS) * nw
            stage_ref[h] = an * cos + pltpu.roll(an, DIL_HEAD_DIM // 2, 1) * sin

    rows = stage_ref.shape[1] // dil
    for p in range(dil):
        for h in range(DIL_HEADS):
            lo = p * DIL_COLS + h * DIL_HEAD_DIM
            src = stage_ref[h] if dil == 1 else stage_ref[h, pl.ds(p, rows, stride=dil), :]
            o_ref[:, lo:lo + DIL_HEAD_DIM] = src.astype(o_ref.dtype)


def _nmm_common_specs(tm):
    return [
        pl.BlockSpec((tm, D_MODEL), lambda i, j: (i, 0), pipeline_mode=pl.Buffered(1)),
        pl.BlockSpec((1, D_MODEL), lambda i, j: (0, 0)),
        pl.BlockSpec((1, D_MODEL), lambda i, j: (0, 0)),
        pl.BlockSpec((1, D_MODEL), lambda i, j: (0, 0)),
    ]


def norm_mod_matmul(x, g, scale, shift, w, *, tm, tn, out_dtype):
    m, n = x.shape[0], w.shape[1]
    return pl.pallas_call(
        _nmm_plain_kernel,
        grid=(m // tm, n // tn),
        in_specs=_nmm_common_specs(tm) + [pl.BlockSpec((D_MODEL, tn), lambda i, j: (0, j))],
        out_specs=pl.BlockSpec((tm, tn), lambda i, j: (i, j)),
        out_shape=jax.ShapeDtypeStruct((m, n), out_dtype),
        scratch_shapes=[pltpu.VMEM((tm, D_MODEL), BF16)],
        compiler_params=_params(2),
        name="norm_mod_matmul",
    )(x, g, scale, shift, w)


def norm_mod_matmul_glu(x, g, scale, shift, w, *, tm, tn, out_dtype):
    m, n = x.shape[0], w.shape[1] // 2
    nj = n // tn
    return pl.pallas_call(
        _nmm_glu_kernel,
        grid=(m // tm, nj),
        in_specs=_nmm_common_specs(tm) + [
            pl.BlockSpec((D_MODEL, tn), lambda i, j: (0, j)),
            pl.BlockSpec((D_MODEL, tn), lambda i, j: (0, j + nj)),
        ],
        out_specs=pl.BlockSpec((tm, tn), lambda i, j: (i, j)),
        out_shape=jax.ShapeDtypeStruct((m, n), out_dtype),
        scratch_shapes=[pltpu.VMEM((tm, D_MODEL), BF16)],
        compiler_params=_params(2),
        name="norm_mod_matmul_glu",
    )(x, g, scale, shift, w, w)


def norm_mod_matmul_dil(x, g, scale, shift, w, nw, qs, cos, sin, *, group, dil, tm):
    m = x.shape[0]
    tn = DIL_COLS
    return pl.pallas_call(
        functools.partial(_nmm_dil_kernel, dil=dil),
        grid=(m // tm, 3),
        in_specs=_nmm_common_specs(tm) + [
            pl.BlockSpec((D_MODEL, tn), lambda i, j: (0, 3 * group + j)),
            pl.BlockSpec((None, 1, DIL_HEAD_DIM), lambda i, j: (3 * group + j, 0, 0)),
            pl.BlockSpec((None, 1, DIL_HEAD_DIM), lambda i, j: (3 * group + j, 0, 0)),
            pl.BlockSpec((tm, DIL_HEAD_DIM), lambda i, j: (i, 0)),
            pl.BlockSpec((tm, DIL_HEAD_DIM), lambda i, j: (i, 0)),
        ],
        out_specs=pl.BlockSpec((tm // dil, dil * tn), lambda i, j: (i, j)),
        out_shape=jax.ShapeDtypeStruct((m // dil, 3 * dil * tn), BF16),
        scratch_shapes=[pltpu.VMEM((tm, D_MODEL), BF16),
                        pltpu.VMEM((DIL_HEADS, tm, DIL_HEAD_DIM), F32)],
        compiler_params=_params(2),
        name=f"norm_mod_matmul_dil_g{group}",
    )(x, g, scale, shift, w, nw, qs, cos, sin)


def _mmres_kernel(a_ref, w_ref, x_ref, gate_ref, o_ref):
    y = _bdot(a_ref[...], w_ref[...].astype(BF16))
    o_ref[...] = x_ref[...] + (1.0 + gate_ref[...]) * y


def matmul_residual(a, w, x, gate, *, tm, tn):
    m, k = a.shape
    n = w.shape[1]
    return pl.pallas_call(
        _mmres_kernel,
        grid=(m // tm, n // tn),
        in_specs=[
            pl.BlockSpec((tm, k), lambda i, j: (i, 0)),
            pl.BlockSpec((k, tn), lambda i, j: (0, j)),
            pl.BlockSpec((tm, tn), lambda i, j: (i, j)),
            pl.BlockSpec((1, tn), lambda i, j: (0, j)),
        ],
        out_specs=pl.BlockSpec((tm, tn), lambda i, j: (i, j)),
        out_shape=jax.ShapeDtypeStruct((m, n), F32),
        compiler_params=_params(2),
        name="matmul_residual",
    )(a, w, x, gate)


def _conv_kernel(prev_ref, cur_ref, next_ref, dw_ref, lg_ref, lb_ref, o_ref, ext_ref, cv_ref, zs_ref,
                 *, tr):
    i = pl.program_id(0)
    last = pl.num_programs(0) - 1
    ext_ref[0:CONV_HALO, :] = jnp.where(i > 0, prev_ref[...], 0.0)
    ext_ref[CONV_HALO:CONV_HALO + tr, :] = cur_ref[...]
    ext_ref[CONV_HALO + tr:, :] = jnp.where(i < last, next_ref[...], 0.0)
    base = CONV_HALO - CONV_WIDTH // 2
    n_chunks = D_MODEL // 128

    def conv_chunk(c, carry):
        cs = pl.ds(pl.multiple_of(c * 128, 128), 128)
        for r0 in range(0, tr, CONV_ROWS):
            acc = None
            for b in range(8):
                z = None
                for k in range(CONV_WIDTH):
                    if (base + k) % 8 != b:
                        continue
                    off = r0 + base + k - b
                    term = ext_ref[off:off + CONV_ROWS + 8, cs] * dw_ref[k:k + 1, cs]
                    z = term if z is None else z + term
                if b == 0:
                    part = z[0:CONV_ROWS]
                else:
                    slot = (r0 // CONV_ROWS) * 8 + b
                    zs_ref[slot] = z
                    part = zs_ref[slot, b:b + CONV_ROWS, :]
                acc = part if acc is None else acc + part
            cv_ref[r0:r0 + CONV_ROWS, cs] = acc
        return carry

    lax.fori_loop(0, n_chunks, conv_chunk, 0)

    def sum_chunk(c, s1):
        cs = pl.ds(pl.multiple_of(c * 128, 128), 128)
        return s1 + jnp.sum(cv_ref[:, cs], axis=1, keepdims=True)

    s1 = lax.fori_loop(0, n_chunks, sum_chunk, jnp.zeros((tr, 1), F32))
    mu = s1 * (1.0 / D_MODEL)

    def var_chunk(c, s2):
        cs = pl.ds(pl.multiple_of(c * 128, 128), 128)
        dlt = cv_ref[:, cs] - mu
        return s2 + jnp.sum(dlt * dlt, axis=1, keepdims=True)

    s2 = lax.fori_loop(0, n_chunks, var_chunk, jnp.zeros((tr, 1), F32))
    rstd = lax.rsqrt(s2 * (1.0 / D_MODEL) + EPS)

    def out_chunk(c, carry):
        cs = pl.ds(pl.multiple_of(c * 128, 128), 128)
        un = (cv_ref[:, cs] - mu) * rstd * lg_ref[:, cs] + lb_ref[:, cs]
        o_ref[:, cs] = (un * jax.nn.sigmoid(un)).astype(o_ref.dtype)
        return carry

    lax.fori_loop(0, n_chunks, out_chunk, 0)


def conv_ln_swish(u, dw, ln_g, ln_b, *, tr):
    m = u.shape[0]
    hb = tr // CONV_HALO
    n_hb = m // CONV_HALO
    return pl.pallas_call(
        functools.partial(_conv_kernel, tr=tr),
        grid=(m // tr,),
        in_specs=[
            pl.BlockSpec((CONV_HALO, D_MODEL), lambda i: (jnp.maximum(i * hb - 1, 0), 0)),
            pl.BlockSpec((tr, D_MODEL), lambda i: (i, 0)),
            pl.BlockSpec((CONV_HALO, D_MODEL), lambda i: (jnp.minimum((i + 1) * hb, n_hb - 1), 0)),
            pl.BlockSpec((CONV_WIDTH, D_MODEL), lambda i: (0, 0)),
            pl.BlockSpec((1, D_MODEL), lambda i: (0, 0)),
            pl.BlockSpec((1, D_MODEL), lambda i: (0, 0)),
        ],
        out_specs=pl.BlockSpec((tr, D_MODEL), lambda i: (i, 0)),
        out_shape=jax.ShapeDtypeStruct((m, D_MODEL), BF16),
        scratch_shapes=[
            pltpu.VMEM((tr + 2 * CONV_HALO, D_MODEL), F32),
            pltpu.VMEM((tr, D_MODEL), F32),
            pltpu.VMEM((8 * (tr // CONV_ROWS), CONV_ROWS + 8, 128), F32),
        ],
        compiler_params=_params(1),
        name="conv_ln_swish",
    )(u, u, u, dw, ln_g, ln_b)


def _pool_kernel(prev_ref, cur_ref, next_ref, wg_ref, cs_ref, o_ref, ext_ref, mix_ref, *, tr):
    i = pl.program_id(0)
    last = pl.num_programs(0) - 1
    ext_ref[0:POOL_HALO, :] = jnp.where(i > 0, prev_ref[...], 0.0)
    ext_ref[POOL_HALO:POOL_HALO + tr, :] = cur_ref[...]
    ext_ref[POOL_HALO + tr:, :] = jnp.where(i < last, next_ref[...], 0.0)
    t = i * tr + lax.broadcasted_iota(jnp.int32, (tr, 1), 0)
    for g, window in enumerate(POOL_WINDOWS):
        r = window // 2
        cnt = (jnp.minimum(t + r + 1, SEQ) - jnp.maximum(t - r, 0)).astype(F32)
        inv_cnt = 1.0 / cnt
        for c in range(POOL_GROUP_DIM // 128):
            lo = g * POOL_GROUP_DIM + c * 128
            acc = jnp.zeros((tr, 128), F32)
            for dlt in range(-r, r + 1):
                acc = acc + ext_ref[POOL_HALO + dlt:POOL_HALO + dlt + tr, lo:lo + 128]
            mix = acc * inv_cnt - ext_ref[POOL_HALO:POOL_HALO + tr, lo:lo + 128]
            mix_ref[:, lo:lo + 128] = mix.astype(BF16)
    for g in range(len(POOL_WINDOWS)):
        cols = slice(g * POOL_GROUP_DIM, (g + 1) * POOL_GROUP_DIM)
        y = _bdot(mix_ref[:, cols], wg_ref[g].astype(BF16))
        o_ref[:, cols] = (y * cs_ref[:, cols]).astype(o_ref.dtype)


def pool_mix(u, w_grp, ch_scale, *, tr):
    m = u.shape[0]
    hb = tr // POOL_HALO
    n_hb = m // POOL_HALO
    n_grp = len(POOL_WINDOWS)
    return pl.pallas_call(
        functools.partial(_pool_kernel, tr=tr),
        grid=(m // tr,),
        in_specs=[
            pl.BlockSpec((POOL_HALO, D_MODEL), lambda i: (jnp.maximum(i * hb - 1, 0), 0)),
            pl.BlockSpec((tr, D_MODEL), lambda i: (i, 0)),
            pl.BlockSpec((POOL_HALO, D_MODEL), lambda i: (jnp.minimum((i + 1) * hb, n_hb - 1), 0)),
            pl.BlockSpec((n_grp, POOL_GROUP_DIM, POOL_GROUP_DIM), lambda i: (0, 0, 0)),
            pl.BlockSpec((1, D_MODEL), lambda i: (0, 0)),
        ],
        out_specs=pl.BlockSpec((tr, D_MODEL), lambda i: (i, 0)),
        out_shape=jax.ShapeDtypeStruct((m, D_MODEL), BF16),
        scratch_shapes=[
            pltpu.VMEM((tr + 2 * POOL_HALO, D_MODEL), F32),
            pltpu.VMEM((tr, D_MODEL), BF16),
        ],
        compiler_params=_params(1),
        name="pool_mix",
    )(u, u, u, w_grp, ch_scale)


def _dil_attn_kernel(q_ref, kp_ref, kc_ref, kn_ref, vp_ref, vc_ref, vn_ref,
                     o_ref, m_ref, l_ref, *, tq, n_slots):
    i = pl.program_id(1)
    row = lax.broadcasted_iota(jnp.int32, (tq, 3 * tq), 0)
    col = lax.broadcasted_iota(jnp.int32, (tq, 3 * tq), 1)
    key_slot = i * tq - tq + col
    valid = (jnp.abs(col - tq - row) <= DIL_HALF) & (key_slot >= 0) & (key_slot < n_slots)
    for h in range(DIL_HEADS):
        hs = slice(h * DIL_HEAD_DIM, (h + 1) * DIL_HEAD_DIM)
        q = q_ref[:, hs]
        s = jnp.concatenate(
            [_bdot_t(q, kp_ref[:, hs]), _bdot_t(q, kc_ref[:, hs]), _bdot_t(q, kn_ref[:, hs])], axis=1)
        s = jnp.where(valid, s, -jnp.inf)
        m = jnp.max(s, axis=1, keepdims=True)
        p = jnp.exp(s - m)
        l = jnp.sum(p, axis=1, keepdims=True)
        pb = p.astype(BF16)
        o = (_bdot(pb[:, 0:tq], vp_ref[:, hs]) + _bdot(pb[:, tq:2 * tq], vc_ref[:, hs])
             + _bdot(pb[:, 2 * tq:], vn_ref[:, hs]))
        o_ref[:, hs] = o
        m_ref[:, hs] = jnp.broadcast_to(m, (tq, DIL_HEAD_DIM))
        l_ref[:, hs] = jnp.broadcast_to(l, (tq, DIL_HEAD_DIM))


def dil_group_attention(qkv, group, dil, *, tq):
    n_slots = SEQ // dil
    nb = n_slots // tq

    def spec(slab, shift):
        return pl.BlockSpec(
            (tq, DIL_COLS), lambda p, i: (jnp.clip(i + shift, 0, nb - 1), slab * dil + p))

    out_spec = pl.BlockSpec((tq, DIL_COLS), lambda p, i: (i, p))
    out_sds = jax.ShapeDtypeStruct((n_slots, dil * DIL_COLS), F32)
    return pl.pallas_call(
        functools.partial(_dil_attn_kernel, tq=tq, n_slots=n_slots),
        grid=(dil, nb),
        in_specs=[spec(0, 0),
                  spec(1, -1), spec(1, 0), spec(1, 1),
                  spec(2, -1), spec(2, 0), spec(2, 1)],
        out_specs=[out_spec, out_spec, out_spec],
        out_shape=[out_sds, out_sds, out_sds],
        compiler_params=_params(2),
        name=f"dil_attn_g{group}",
    )(qkv, qkv, qkv, qkv, qkv, qkv, qkv)


def _dil_merge_kernel(*refs, dils):
    n_in = 3 * len(dils)
    in_refs, out_ref, scratch = refs[:n_in], refs[n_in], list(refs[n_in + 1:])
    tm = out_ref.shape[0]
    for h in range(DIL_HEADS):
        tok = []
        for gi, dil in enumerate(dils):
            for a, src in enumerate(in_refs[3 * gi:3 * gi + 3]):
                if dil == 1:
                    tok.append(src[:, h * DIL_HEAD_DIM:(h + 1) * DIL_HEAD_DIM])
                    continue
                dst = scratch.pop(0)
                for p in range(dil):
                    lo = p * DIL_COLS + h * DIL_HEAD_DIM
                    dst[h, pl.ds(p, tm // dil, stride=dil), :] = src[:, lo:lo + DIL_HEAD_DIM]
                tok.append(dst[h])
                scratch.append(dst)
        o0, m0, l0, o1, m1, l1, o2, m2, l2 = tok
        mm = jnp.maximum(jnp.maximum(m0, m1), m2)
        w0 = jnp.exp(m0 - mm)
        w1 = jnp.exp(m1 - mm)
        w2 = jnp.exp(m2 - mm)
        num = w0 * o0 + w1 * o1 + w2 * o2
        den = w0 * l0 + w1 * l1 + w2 * l2
        out_ref[:, h * DIL_HEAD_DIM:(h + 1) * DIL_HEAD_DIM] = (num / den).astype(out_ref.dtype)


def dil_merge(parts, dils, *, tm):
    in_specs = []
    n_scratch = 0
    for dil in dils:
        in_specs += [pl.BlockSpec((tm // dil, dil * DIL_COLS), lambda i: (i, 0))] * 3
        n_scratch += 3 if dil > 1 else 0
    flat = [a for part in parts for a in part]
    return pl.pallas_call(
        functools.partial(_dil_merge_kernel, dils=tuple(dils)),
        grid=(SEQ // tm,),
        in_specs=in_specs,
        out_specs=pl.BlockSpec((tm, DIL_COLS), lambda i: (i, 0)),
        out_shape=jax.ShapeDtypeStruct((SEQ, DIL_COLS), BF16),
        scratch_shapes=[pltpu.VMEM((DIL_HEADS, tm, DIL_HEAD_DIM), F32)] * n_scratch,
        compiler_params=_params(1),
        name="dil_merge",
    )(*flat)


def _mla_prep_kernel(u_ref, qan_ref, kvan_ref, wq_ref, wkv_ref, qn_ref, kn_ref,
                     cos_ref, sa_ref, sb_ref, q_out, k_out, vt_out, cq_s, ckv_s):
    @pl.when(pl.program_id(1) == 0)
    def _():
        cq = u_ref[:, 0:MLA_Q_RANK]
        cq_s[...] = (cq * lax.rsqrt(jnp.mean(cq * cq, axis=-1, keepdims=True) + EPS)
                     * qan_ref[...]).astype(BF16)
        ckv = u_ref[:, MLA_Q_RANK:MLA_Q_RANK + MLA_KV_RANK]
        ckv_s[...] = (ckv * lax.rsqrt(jnp.mean(ckv * ckv, axis=-1, keepdims=True) + EPS)
                      * kvan_ref[...]).astype(BF16)

    cos = cos_ref[...]
    sa = sa_ref[...]
    sb = sb_ref[...]

    def norm_rope(nope, rp, w_ref):
        ssq = jnp.sum(nope * nope, axis=-1, keepdims=True) + jnp.sum(rp * rp, axis=-1, keepdims=True)
        r = lax.rsqrt(ssq * (1.0 / MLA_QK) + EPS)
        nope_n = nope * r * w_ref[:, 0:128]
        rp_n = rp * r * w_ref[:, 128:256]
        rp_r = rp_n * cos + pltpu.roll(rp_n, 96, 1) * sa + pltpu.roll(rp_n, 32, 1) * sb
        return nope_n, rp_r

    q = _bdot(cq_s[...], wq_ref[...].astype(BF16))
    kv = _bdot(ckv_s[...], wkv_ref[...].astype(BF16))
    kr = u_ref[:, MLA_Q_RANK + MLA_KV_RANK:MLA_Q_RANK + MLA_KV_RANK + 128]
    scale = MLA_QK ** -0.5 * LOG2_E
    qn, qr = norm_rope(q[:, 0:128], q[:, 128:256], qn_ref)
    q_out[:, 0:128] = (qn * scale).astype(BF16)
    q_out[:, 128:256] = (qr * scale).astype(BF16)
    kn, kr_r = norm_rope(kv[:, 0:128], kr, kn_ref)
    k_out[:, 0:128] = kn.astype(BF16)
    k_out[:, 128:256] = kr_r.astype(BF16)
    vt_out[...] = kv[:, 128:256].T.astype(BF16)


def mla_prep(u, q_a_norm, kv_a_norm, wq_pad, w_kv_up, qn_pad, kn_pad, cos, sa, sb, *, tm):
    m = u.shape[0]
    ucols = u.shape[1]
    vec = lambda n: pl.BlockSpec((1, n), lambda i, h: (0, 0))
    tab = pl.BlockSpec((tm, 128), lambda i, h: (i, 0))
    head_out = lambda w: pl.BlockSpec((None, tm, w), lambda i, h: (h, i, 0))
    return pl.pallas_call(
        _mla_prep_kernel,
        grid=(m // tm, MLA_HEADS),
        in_specs=[
            pl.BlockSpec((tm, ucols), lambda i, h: (i, 0)),
            vec(MLA_Q_RANK), vec(MLA_KV_RANK),
            pl.BlockSpec((MLA_Q_RANK, MLA_PAD), lambda i, h: (0, h)),
            pl.BlockSpec((MLA_KV_RANK, MLA_NOPE + MLA_V), lambda i, h: (0, h)),
            vec(MLA_PAD), vec(MLA_PAD),
            tab, tab, tab,
        ],
        out_specs=[head_out(MLA_PAD), head_out(MLA_PAD),
                   pl.BlockSpec((None, MLA_V, tm), lambda i, h: (h, 0, i))],
        out_shape=[
            jax.ShapeDtypeStruct((MLA_HEADS, m, MLA_PAD), BF16),
            jax.ShapeDtypeStruct((MLA_HEADS, m, MLA_PAD), BF16),
            jax.ShapeDtypeStruct((MLA_HEADS, MLA_V, m), BF16),
        ],
        scratch_shapes=[pltpu.VMEM((tm, MLA_Q_RANK), BF16), pltpu.VMEM((tm, MLA_KV_RANK), BF16)],
        compiler_params=_params(2),
        name="mla_prep",
    )(u, q_a_norm, kv_a_norm, wq_pad, w_kv_up, qn_pad, kn_pad, cos, sa, sb)


def _mla_attn_kernel(q_ref, k_ref, vt_ref, o_ref, m_s, l_s, acc_s, sa_ref, sb_ref, *, tk):
    q = q_ref[...]
    n_chunks = k_ref.shape[0] // tk
    assert n_chunks % 2 == 0
    m_s[...] = jnp.full(m_s.shape, -jnp.inf, F32)
    l_s[...] = jnp.zeros(l_s.shape, F32)
    acc_s[...] = jnp.zeros(acc_s.shape, F32)

    def key_rows(c):
        return pl.ds(pl.multiple_of(c * tk, tk), tk)

    def scores(c, dst):
        dst[...] = _bdot_t(k_ref[key_rows(c), :], q)

    def consume(c, src):
        st = src[...]
        m_prev = m_s[...]
        m_new = jnp.maximum(m_prev, jnp.max(st, axis=0, keepdims=True))
        alpha = jnp.exp2(m_prev - m_new)
        pt = jnp.exp2(st - m_new)
        l_s[...] = alpha * l_s[...] + jnp.sum(pt, axis=0, keepdims=True)
        acc_s[...] = alpha * acc_s[...] + _bdot(vt_ref[:, key_rows(c)], pt.astype(BF16))
        m_s[...] = m_new

    scores(0, sa_ref)

    def body(c2, carry):
        c = 2 * c2
        scores(c + 1, sb_ref)
        consume(c, sa_ref)
        scores(c + 2, sa_ref)
        consume(c + 1, sb_ref)
        return carry

    lax.fori_loop(0, n_chunks // 2 - 1, body, 0)
    scores(n_chunks - 1, sb_ref)
    consume(n_chunks - 2, sa_ref)
    consume(n_chunks - 1, sb_ref)
    o_ref[...] = (acc_s[...] / l_s[...]).T.astype(o_ref.dtype)


def mla_attention(q, k, vt, *, tq, tk):
    n_heads, m, _ = q.shape
    return pl.pallas_call(
        functools.partial(_mla_attn_kernel, tk=tk),
        grid=(n_heads, m // tq),
        in_specs=[
            pl.BlockSpec((None, tq, MLA_PAD), lambda h, i: (h, i, 0)),
            pl.BlockSpec((None, m, MLA_PAD), lambda h, i: (h, 0, 0)),
            pl.BlockSpec((None, MLA_V, m), lambda h, i: (h, 0, 0)),
        ],
        out_specs=pl.BlockSpec((tq, MLA_V), lambda h, i: (i, h)),
        out_shape=jax.ShapeDtypeStruct((m, n_heads * MLA_V), BF16),
        scratch_shapes=[
            pltpu.VMEM((1, tq), F32), pltpu.VMEM((1, tq), F32), pltpu.VMEM((MLA_V, tq), F32),
            pltpu.VMEM((tk, tq), F32), pltpu.VMEM((tk, tq), F32)],
        compiler_params=_params(2),
        name="mla_attention",
    )(q, k, vt)


def _router_kernel(x_ref, g_ref, sc_ref, sh_ref, wr_ref, hn_out, aff_out):
    hn = _norm_mod(x_ref[...], g_ref[...], sc_ref[...], sh_ref[...])
    hn_out[...] = hn.astype(BF16)
    logits = jnp.dot(hn, wr_ref[...], preferred_element_type=F32, precision=lax.Precision.HIGHEST)
    z = jnp.exp(logits - jnp.max(logits, axis=-1, keepdims=True))
    aff_out[...] = z / jnp.sum(z, axis=-1, keepdims=True)


def moe_router(x, g, scale, shift, w_router, *, tm):
    m = x.shape[0]
    vec = pl.BlockSpec((1, D_MODEL), lambda i: (0, 0))
    return pl.pallas_call(
        _router_kernel,
        grid=(m // tm,),
        in_specs=[pl.BlockSpec((tm, D_MODEL), lambda i: (i, 0)), vec, vec, vec,
                  pl.BlockSpec((D_MODEL, N_EXPERTS), lambda i: (0, 0))],
        out_specs=[pl.BlockSpec((tm, D_MODEL), lambda i: (i, 0)),
                   pl.BlockSpec((tm, N_EXPERTS), lambda i: (i, 0))],
        out_shape=[jax.ShapeDtypeStruct((m, D_MODEL), BF16),
                   jax.ShapeDtypeStruct((m, N_EXPERTS), F32)],
        compiler_params=_params(1),
        name="moe_router",
    )(x, g, scale, shift, w_router)


def _ffn_kernel(xs_ref, wg_ref, wu_ref, wd_ref, gt_ref, rg_ref, o_ref):
    f = pl.program_id(1)
    xs = xs_ref[...]
    a = _bdot(xs, wg_ref[...].astype(BF16))
    b = _bdot(xs, wu_ref[...].astype(BF16))
    hmid = (a * jax.nn.sigmoid(a) * b).astype(BF16)
    y = _bdot(hmid, wd_ref[...].astype(BF16))
    last = f == pl.num_programs(1) - 1

    @pl.when(f == 0)
    def _():
        o_ref[...] = y

    @pl.when(jnp.logical_and(f > 0, jnp.logical_not(last)))
    def _():
        o_ref[...] += y

    @pl.when(last)
    def _():
        o_ref[...] = (o_ref[...] + y) * gt_ref[...] * (1.0 + rg_ref[...])


def expert_ffn(xs, w_gate, w_up, w_down, gates, res_gate, *, layer, tf):
    n_exp, cap, _ = xs.shape
    return pl.pallas_call(
        _ffn_kernel,
        grid=(n_exp, EXPERT_FF // tf),
        in_specs=[
            pl.BlockSpec((None, cap, D_MODEL), lambda e, f: (e, 0, 0)),
            pl.BlockSpec((None, None, D_MODEL, tf), lambda e, f: (layer, e, 0, f)),
            pl.BlockSpec((None, None, D_MODEL, tf), lambda e, f: (layer, e, 0, f)),
            pl.BlockSpec((None, None, tf, D_MODEL), lambda e, f: (layer, e, f, 0)),
            pl.BlockSpec((None, cap, 1), lambda e, f: (e, 0, 0)),
            pl.BlockSpec((1, D_MODEL), lambda e, f: (0, 0)),
        ],
        out_specs=pl.BlockSpec((None, cap, D_MODEL), lambda e, f: (e, 0, 0)),
        out_shape=jax.ShapeDtypeStruct((n_exp, cap, D_MODEL), F32),
        compiler_params=_params(2),
        name="expert_ffn",
    )(xs, w_gate, w_up, w_down, gates, res_gate)


def moe_layer(x, g, scale, shift, gate, w_router, w_gate, w_up, w_down, *, layer):
    hn, aff = moe_router(x, g, scale, shift, w_router, tm=512)
    gates, idx = lax.top_k(aff.T, EXPERT_CAP)
    xs = jnp.take(hn, idx.reshape(-1), axis=0).reshape(N_EXPERTS, EXPERT_CAP, D_MODEL)
    y = expert_ffn(xs, w_gate, w_up, w_down, gates[..., None], gate, layer=layer, tf=256)
    return x.at[idx.reshape(-1)].add(y.reshape(-1, D_MODEL))


def _rope_tables(positions, dim):
    half = dim // 2
    inv = ROPE_THETA ** (-jnp.arange(half, dtype=F32) / half)
    ang = positions.astype(F32)[:, None] * inv
    return jnp.cos(ang), jnp.sin(ang)


def _row(v):
    return v.reshape(1, -1)


def conv_mixer(xs, g, shift, scale, gate, w_in, dw, ln_g, ln_b, w_out):
    u = norm_mod_matmul_glu(xs, g, scale, shift, w_in, tm=1024, tn=512, out_dtype=F32)
    v = conv_ln_swish(u, dw, _row(ln_g), _row(ln_b), tr=256)
    return matmul_residual(v, w_out, xs, gate, tm=1024, tn=512)


def pool_mixer(xs, g, shift, scale, gate, w_in, w_grp, ch_scale, w_out):
    u = norm_mod_matmul(xs, g, scale, shift, w_in, tm=1024, tn=512, out_dtype=F32)
    v = pool_mix(u, w_grp, _row(ch_scale), tr=256)
    return matmul_residual(v, w_out, xs, gate, tm=1024, tn=512)


def dil_mixer(xs, g, shift, scale, gate, pos, w_in, q_norm, k_norm, w_out):
    cos, sin = _rope_tables(pos, DIL_HEAD_DIM)
    cos_f = jnp.concatenate([cos, cos], axis=1)
    sin_f = jnp.concatenate([-sin, sin], axis=1)
    ones = jnp.ones((DIL_HEAD_DIM,), F32)
    nw = jnp.stack([w for grp in range(len(DIL_PATTERNS))
                    for w in (q_norm[grp], k_norm[grp], ones)])[:, None, :]
    qscale = jnp.asarray([DIL_HEAD_DIM ** -0.5, 1.0, 1.0] * len(DIL_PATTERNS), F32)
    qs = jnp.broadcast_to(qscale[:, None, None], nw.shape)
    dils = [dil for _, dil in DIL_PATTERNS]
    parts = []
    for grp, dil in enumerate(dils):
        qkv = norm_mod_matmul_dil(xs, g, scale, shift, w_in, nw, qs, cos_f, sin_f,
                                  group=grp, dil=dil, tm=1024)
        parts.append(dil_group_attention(qkv, grp, dil, tq=256))
    o = dil_merge(parts, dils, tm=256)
    return matmul_residual(o, w_out, xs, gate, tm=1024, tn=512)


def mla_mixer(xs, g, shift, scale, gate, pos, w_in, q_a_norm, w_q_up, kv_a_norm, w_kv_up,
              q_norm, k_norm, w_out):
    cos, sin = _rope_tables(pos, MLA_ROPE)
    z32 = jnp.zeros_like(sin)
    z64 = jnp.zeros((SEQ, 64), F32)
    cos_p = jnp.concatenate([cos, cos, z64], axis=1)
    sin_a = jnp.concatenate([-sin, z32, z64], axis=1)
    sin_b = jnp.concatenate([z32, sin, z64], axis=1)
    w_in_pad = jnp.pad(w_in, ((0, 0), (0, 64)))
    wq_pad = jnp.pad(w_q_up.reshape(MLA_Q_RANK, MLA_HEADS, MLA_QK),
                     ((0, 0), (0, 0), (0, MLA_PAD - MLA_QK))).reshape(MLA_Q_RANK, MLA_HEADS * MLA_PAD)
    qn_pad = jnp.pad(q_norm, (0, MLA_PAD - MLA_QK)).reshape(1, MLA_PAD)
    kn_pad = jnp.pad(k_norm, (0, MLA_PAD - MLA_QK)).reshape(1, MLA_PAD)
    u = norm_mod_matmul(xs, g, scale, shift, w_in_pad, tm=512, tn=w_in_pad.shape[1], out_dtype=F32)
    q, k, vt = mla_prep(u, _row(q_a_norm), _row(kv_a_norm), wq_pad, w_kv_up, qn_pad, kn_pad,
                       cos_p, sin_a, sin_b, tm=512)
    o = mla_attention(q, k, vt, tq=1024, tk=2048)
    return matmul_residual(o, w_out, xs, gate, tm=1024, tn=512)


def kernel(x, c, positions, norm_g, ada_w, ada_b, conv_w_in, conv_dw, conv_ln_g, conv_ln_b, conv_w_out, pool_w_in, pool_w_grp, pool_scale, pool_w_out, dil_w_in, dil_q_norm, dil_k_norm, dil_w_out, mla_w_in, mla_q_a_norm, mla_w_q_up, mla_kv_a_norm, mla_w_kv_up, mla_q_norm, mla_k_norm, mla_w_out, moe_router, moe_w_gate, moe_w_up, moe_w_down):
    assert x.shape == (1, SEQ, D_MODEL) and ada_w.shape[0] == DEPTH
    xs = x.reshape(SEQ, D_MODEL)
    pos = positions.reshape(SEQ)
    mod = ada_all(c, ada_w, ada_b)

    def modulation(layer, sub):
        mrow = mod[2 * layer + sub]
        return mrow[:, 0:D_MODEL], mrow[:, D_MODEL:2 * D_MODEL], mrow[:, 2 * D_MODEL:]

    for layer in range(DEPTH):
        mixer = layer % 4
        occ = layer // 4
        shift, scale, gate = modulation(layer, 0)
        g = _row(norm_g[layer, 0])
        if mixer == 0:
            xs = conv_mixer(xs, g, shift, scale, gate, conv_w_in[occ], conv_dw[occ], conv_ln_g[occ],
                            conv_ln_b[occ], conv_w_out[occ])
        elif mixer == 1:
            xs = pool_mixer(xs, g, shift, scale, gate, pool_w_in[occ], pool_w_grp[occ], pool_scale[occ],
                            pool_w_out[occ])
        elif mixer == 2:
            xs = dil_mixer(xs, g, shift, scale, gate, pos, dil_w_in[occ], dil_q_norm[occ], dil_k_norm[occ],
                           dil_w_out[occ])
        else:
            xs = mla_mixer(xs, g, shift, scale, gate, pos, mla_w_in[occ], mla_q_a_norm[occ], mla_w_q_up[occ],
                           mla_kv_a_norm[occ], mla_w_kv_up[occ], mla_q_norm[occ], mla_k_norm[occ],
                           mla_w_out[occ])
        shift, scale, gate = modulation(layer, 1)
        xs = moe_layer(xs, _row(norm_g[layer, 1]), scale, shift, gate, moe_router[layer],
                       moe_w_gate, moe_w_up, moe_w_down, layer=layer)
    return xs.reshape(1, SEQ, D_MODEL)
```

```python
import functools

import jax
import jax.numpy as jnp
from jax import lax
from jax.experimental import pallas as pl
from jax.experimental.pallas import tpu as pltpu

F32 = jnp.float32
BF16 = jnp.bfloat16

D_MODEL = 2048
SEQ = 8192
DEPTH = 4
EPS = 1e-6
ROPE_THETA = 10000.0
LOG2_E = 1.4426950408889634

CONV_WIDTH = 31
CONV_HALO = 16
CONV_ROWS = 128
POOL_WINDOWS = (2, 4, 8, 16)
POOL_GROUP_DIM = D_MODEL // len(POOL_WINDOWS)
POOL_HALO = 8

DIL_PATTERNS = ((128, 1), (512, 4), (2048, 16))
DIL_HEADS = 8
DIL_HEAD_DIM = 128
DIL_COLS = DIL_HEADS * DIL_HEAD_DIM
DIL_HALF = 64

MLA_HEADS = 16
MLA_Q_RANK = 512
MLA_KV_RANK = 512
MLA_NOPE = 128
MLA_ROPE = 64
MLA_V = 128
MLA_QK = MLA_NOPE + MLA_ROPE
MLA_PAD = 256

N_EXPERTS = 16
EXPERT_FF = 1024
EXPERT_CAP = 2 * SEQ // N_EXPERTS

VMEM_LIMIT_BYTES = 56 * 1024 * 1024


def _params(n_axes):
    return pltpu.CompilerParams(
        dimension_semantics=("arbitrary",) * n_axes, vmem_limit_bytes=VMEM_LIMIT_BYTES)


def _bdot(a, b):
    return jnp.dot(a, b, preferred_element_type=F32)


def _bdot_t(a, b):
    return lax.dot_general(a, b, (((1,), (1,)), ((), ())), preferred_element_type=F32)


def _ada_kernel(c_ref, w_ref, b_ref, o_ref):
    c = c_ref[...]
    cond = c * jax.nn.sigmoid(c)
    o_ref[...] = _bdot(cond.astype(BF16), w_ref[...].astype(BF16)) + b_ref[...]


def ada_all(c, ada_w, ada_b):
    n = ada_w.shape[0] * ada_w.shape[1]
    w = ada_w.reshape(n, D_MODEL, 3 * D_MODEL)
    b = ada_b.reshape(n, 1, 3 * D_MODEL)
    c8 = jnp.broadcast_to(c.reshape(1, D_MODEL), (8, D_MODEL))
    tn = 1024
    out = pl.pallas_call(
        _ada_kernel,
        grid=(n, 3 * D_MODEL // tn),
        in_specs=[
            pl.BlockSpec((8, D_MODEL), lambda l, j: (0, 0)),
            pl.BlockSpec((None, D_MODEL, tn), lambda l, j: (l, 0, j)),
            pl.BlockSpec((None, 1, tn), lambda l, j: (l, 0, j)),
        ],
        out_specs=pl.BlockSpec((None, 8, tn), lambda l, j: (l, 0, j)),
        out_shape=jax.ShapeDtypeStruct((n, 8, 3 * D_MODEL), F32),
        compiler_params=_params(2),
        name="ada_all",
    )(c8, w, b)
    return out[:, 0:1, :]


def _norm_mod(x, g, scale, shift):
    ms = jnp.mean(x * x, axis=-1, keepdims=True)
    y = x * lax.rsqrt(ms + EPS) * g
    return y * (1.0 + scale) + shift


def _nmm_plain_kernel(x_ref, g_ref, sc_ref, sh_ref, w_ref, o_ref, hn_ref):
    @pl.when(pl.program_id(1) == 0)
    def _():
        hn_ref[...] = _norm_mod(x_ref[...], g_ref[...], sc_ref[...], sh_ref[...]).astype(BF16)

    o_ref[...] = _bdot(hn_ref[...], w_ref[...].astype(BF16)).astype(o_ref.dtype)


def _nmm_glu_kernel(x_ref, g_ref, sc_ref, sh_ref, wa_ref, wb_ref, o_ref, hn_ref):
    @pl.when(pl.program_id(1) == 0)
    def _():
        hn_ref[...] = _norm_mod(x_ref[...], g_ref[...], sc_ref[...], sh_ref[...]).astype(BF16)

    hn = hn_ref[...]
    a = _bdot(hn, wa_ref[...].astype(BF16))
    b = _bdot(hn, wb_ref[...].astype(BF16))
    o_ref[...] = (a * jax.nn.sigmoid(b)).astype(o_ref.dtype)


def _nmm_dil_kernel(x_ref, g_ref, sc_ref, sh_ref, w_ref, nw_ref, qs_ref, cos_ref, sin_ref,
                    o_ref, hn_ref, stage_ref, *, dil):
    j = pl.program_id(1)

    @pl.when(j == 0)
    def _():
        hn_ref[...] = _norm_mod(x_ref[...], g_ref[...], sc_ref[...], sh_ref[...]).astype(BF16)

    acc = _bdot(hn_ref[...], w_ref[...].astype(BF16))
    is_v = j == 2

    @pl.when(is_v)
    def _():
        for h in range(DIL_HEADS):
            stage_ref[h] = acc[:, h * DIL_HEAD_DIM:(h + 1) * DIL_HEAD_DIM]

    @pl.when(jnp.logical_not(is_v))
    def _():
        nw = nw_ref[...] * qs_ref[...]
        cos = cos_ref[...]
        sin = sin_ref[...]
        for h in range(DIL_HEADS):
            a = acc[:, h * DIL_HEAD_DIM:(h + 1) * DIL_HEAD_DIM]
            ms = jnp.mean(a * a, axis=-1, keepdims=True)
            an = a * lax.rsqrt(ms + EPS) * nw
            stage_ref[h] = an * cos + pltpu.roll(an, DIL_HEAD_DIM // 2, 1) * sin

    rows = stage_ref.shape[1] // dil
    for p in range(dil):
        for h in range(DIL_HEADS):
            lo = p * DIL_COLS + h * DIL_HEAD_DIM
            src = stage_ref[h] if dil == 1 else stage_ref[h, pl.ds(p, rows, stride=dil), :]
            o_ref[:, lo:lo + DIL_HEAD_DIM] = src.astype(o_ref.dtype)


def _nmm_common_specs(tm):
    return [
        pl.BlockSpec((tm, D_MODEL), lambda i, j: (i, 0), pipeline_mode=pl.Buffered(1)),
        pl.BlockSpec((1, D_MODEL), lambda i, j: (0, 0)),
        pl.BlockSpec((1, D_MODEL), lambda i, j: (0, 0)),
        pl.BlockSpec((1, D_MODEL), lambda i, j: (0, 0)),
    ]


def norm_mod_matmul(x, g, scale, shift, w, *, tm, tn, out_dtype):
    m, n = x.shape[0], w.shape[1]
    return pl.pallas_call(
        _nmm_plain_kernel,
        grid=(m // tm, n // tn),
        in_specs=_nmm_common_specs(tm) + [pl.BlockSpec((D_MODEL, tn), lambda i, j: (0, j))],
        out_specs=pl.BlockSpec((tm, tn), lambda i, j: (i, j)),
        out_shape=jax.ShapeDtypeStruct((m, n), out_dtype),
        scratch_shapes=[pltpu.VMEM((tm, D_MODEL), BF16)],
        compiler_params=_params(2),
        name="norm_mod_matmul",
    )(x, g, scale, shift, w)


def norm_mod_matmul_glu(x, g, scale, shift, w, *, tm, tn, out_dtype):
    m, n = x.shape[0], w.shape[1] // 2
    nj = n // tn
    return pl.pallas_call(
        _nmm_glu_kernel,
        grid=(m // tm, nj),
        in_specs=_nmm_common_specs(tm) + [
            pl.BlockSpec((D_MODEL, tn), lambda i, j: (0, j)),
            pl.BlockSpec((D_MODEL, tn), lambda i, j: (0, j + nj)),
        ],
        out_specs=pl.BlockSpec((tm, tn), lambda i, j: (i, j)),
        out_shape=jax.ShapeDtypeStruct((m, n), out_dtype),
        scratch_shapes=[pltpu.VMEM((tm, D_MODEL), BF16)],
        compiler_params=_params(2),
        name="norm_mod_matmul_glu",
    )(x, g, scale, shift, w, w)


def norm_mod_matmul_dil(x, g, scale, shift, w, nw, qs, cos, sin, *, group, dil, tm):
    m = x.shape[0]
    tn = DIL_COLS
    return pl.pallas_call(
        functools.partial(_nmm_dil_kernel, dil=dil),
        grid=(m // tm, 3),
        in_specs=_nmm_common_specs(tm) + [
            pl.BlockSpec((D_MODEL, tn), lambda i, j: (0, 3 * group + j)),
            pl.BlockSpec((None, 1, DIL_HEAD_DIM), lambda i, j: (3 * group + j, 0, 0)),
            pl.BlockSpec((None, 1, DIL_HEAD_DIM), lambda i, j: (3 * group + j, 0, 0)),
            pl.BlockSpec((tm, DIL_HEAD_DIM), lambda i, j: (i, 0)),
            pl.BlockSpec((tm, DIL_HEAD_DIM), lambda i, j: (i, 0)),
        ],
        out_specs=pl.BlockSpec((tm // dil, dil * tn), lambda i, j: (i, j)),
        out_shape=jax.ShapeDtypeStruct((m // dil, 3 * dil * tn), BF16),
        scratch_shapes=[pltpu.VMEM((tm, D_MODEL), BF16),
                        pltpu.VMEM((DIL_HEADS, tm, DIL_HEAD_DIM), F32)],
        compiler_params=_params(2),
        name=f"norm_mod_matmul_dil_g{group}",
    )(x, g, scale, shift, w, nw, qs, cos, sin)


def _mmres_kernel(a_ref, w_ref, x_ref, gate_ref, o_ref):
    y = _bdot(a_ref[...], w_ref[...].astype(BF16))
    o_ref[...] = x_ref[...] + (1.0 + gate_ref[...]) * y


def matmul_residual(a, w, x, gate, *, tm, tn):
    m, k = a.shape
    n = w.shape[1]
    return pl.pallas_call(
        _mmres_kernel,
        grid=(m // tm, n // tn),
        in_specs=[
            pl.BlockSpec((tm, k), lambda i, j: (i, 0)),
            pl.BlockSpec((k, tn), lambda i, j: (0, j)),
            pl.BlockSpec((tm, tn), lambda i, j: (i, j)),
            pl.BlockSpec((1, tn), lambda i, j: (0, j)),
        ],
        out_specs=pl.BlockSpec((tm, tn), lambda i, j: (i, j)),
        out_shape=jax.ShapeDtypeStruct((m, n), F32),
        compiler_params=_params(2),
        name="matmul_residual",
    )(a, w, x, gate)


def _conv_kernel(prev_ref, cur_ref, next_ref, dw_ref, lg_ref, lb_ref, o_ref, ext_ref, cv_ref, zs_ref,
                 *, tr):
    i = pl.program_id(0)
    last = pl.num_programs(0) - 1
    ext_ref[0:CONV_HALO, :] = jnp.where(i > 0, prev_ref[...], 0.0)
    ext_ref[CONV_HALO:CONV_HALO + tr, :] = cur_ref[...]
    ext_ref[CONV_HALO + tr:, :] = jnp.where(i < last, next_ref[...], 0.0)
    base = CONV_HALO - CONV_WIDTH // 2
    n_chunks = D_MODEL // 128

    def conv_chunk(c, carry):
        cs = pl.ds(pl.multiple_of(c * 128, 128), 128)
        for r0 in range(0, tr, CONV_ROWS):
            acc = None
            for b in range(8):
                z = None
                for k in range(CONV_WIDTH):
                    if (base + k) % 8 != b:
                        continue
                    off = r0 + base + k - b
                    term = ext_ref[off:off + CONV_ROWS + 8, cs] * dw_ref[k:k + 1, cs]
                    z = term if z is None else z + term
                if b == 0:
                    part = z[0:CONV_ROWS]
                else:
                    slot = (r0 // CONV_ROWS) * 8 + b
                    zs_ref[slot] = z
                    part = zs_ref[slot, b:b + CONV_ROWS, :]
                acc = part if acc is None else acc + part
            cv_ref[r0:r0 + CONV_ROWS, cs] = acc
        return carry

    lax.fori_loop(0, n_chunks, conv_chunk, 0)

    def sum_chunk(c, s1):
        cs = pl.ds(pl.multiple_of(c * 128, 128), 128)
        return s1 + jnp.sum(cv_ref[:, cs], axis=1, keepdims=True)

    s1 = lax.fori_loop(0, n_chunks, sum_chunk, jnp.zeros((tr, 1), F32))
    mu = s1 * (1.0 / D_MODEL)

    def var_chunk(c, s2):
        cs = pl.ds(pl.multiple_of(c * 128, 128), 128)
        dlt = cv_ref[:, cs] - mu
        return s2 + jnp.sum(dlt * dlt, axis=1, keepdims=True)

    s2 = lax.fori_loop(0, n_chunks, var_chunk, jnp.zeros((tr, 1), F32))
    rstd = lax.rsqrt(s2 * (1.0 / D_MODEL) + EPS)

    def out_chunk(c, carry):
        cs = pl.ds(pl.multiple_of(c * 128, 128), 128)
        un = (cv_ref[:, cs] - mu) * rstd * lg_ref[:, cs] + lb_ref[:, cs]
        o_ref[:, cs] = (un * jax.nn.sigmoid(un)).astype(o_ref.dtype)
        return carry

    lax.fori_loop(0, n_chunks, out_chunk, 0)


def conv_ln_swish(u, dw, ln_g, ln_b, *, tr):
    m = u.shape[0]
    hb = tr // CONV_HALO
    n_hb = m // CONV_HALO
    return pl.pallas_call(
        functools.partial(_conv_kernel, tr=tr),
        grid=(m // tr,),
        in_specs=[
            pl.BlockSpec((CONV_HALO, D_MODEL), lambda i: (jnp.maximum(i * hb - 1, 0), 0)),
            pl.BlockSpec((tr, D_MODEL), lambda i: (i, 0)),
            pl.BlockSpec((CONV_HALO, D_MODEL), lambda i: (jnp.minimum((i + 1) * hb, n_hb - 1), 0)),
            pl.BlockSpec((CONV_WIDTH, D_MODEL), lambda i: (0, 0)),
            pl.BlockSpec((1, D_MODEL), lambda i: (0, 0)),
            pl.BlockSpec((1, D_MODEL), lambda i: (0, 0)),
        ],
        out_specs=pl.BlockSpec((tr, D_MODEL), lambda i: (i, 0)),
        out_shape=jax.ShapeDtypeStruct((m, D_MODEL), BF16),
        scratch_shapes=[
            pltpu.VMEM((tr + 2 * CONV_HALO, D_MODEL), F32),
            pltpu.VMEM((tr, D_MODEL), F32),
            pltpu.VMEM((8 * (tr // CONV_ROWS), CONV_ROWS + 8, 128), F32),
        ],
        compiler_params=_params(1),
        name="conv_ln_swish",
    )(u, u, u, dw, ln_g, ln_b)


def _pool_kernel(prev_ref, cur_ref, next_ref, wg_ref, cs_ref, o_ref, ext_ref, mix_ref, *, tr):
    i = pl.program_id(0)
    last = pl.num_programs(0) - 1
    ext_ref[0:POOL_HALO, :] = jnp.where(i > 0, prev_ref[...], 0.0)
    ext_ref[POOL_HALO:POOL_HALO + tr, :] = cur_ref[...]
    ext_ref[POOL_HALO + tr:, :] = jnp.where(i < last, next_ref[...], 0.0)
    t = i * tr + lax.broadcasted_iota(jnp.int32, (tr, 1), 0)
    for g, window in enumerate(POOL_WINDOWS):
        r = window // 2
        cnt = (jnp.minimum(t + r + 1, SEQ) - jnp.maximum(t - r, 0)).astype(F32)
        inv_cnt = 1.0 / cnt
        for c in range(POOL_GROUP_DIM // 128):
            lo = g * POOL_GROUP_DIM + c * 128
            acc = jnp.zeros((tr, 128), F32)
            for dlt in range(-r, r + 1):
                acc = acc + ext_ref[POOL_HALO + dlt:POOL_HALO + dlt + tr, lo:lo + 128]
            mix = acc * inv_cnt - ext_ref[POOL_HALO:POOL_HALO + tr, lo:lo + 128]
            mix_ref[:, lo:lo + 128] = mix.astype(BF16)
    for g in range(len(POOL_WINDOWS)):
        cols = slice(g * POOL_GROUP_DIM, (g + 1) * POOL_GROUP_DIM)
        y = _bdot(mix_ref[:, cols], wg_ref[g].astype(BF16))
        o_ref[:, cols] = (y * cs_ref[:, cols]).astype(o_ref.dtype)


def pool_mix(u, w_grp, ch_scale, *, tr):
    m = u.shape[0]
    hb = tr // POOL_HALO
    n_hb = m // POOL_HALO
    n_grp = len(POOL_WINDOWS)
    return pl.pallas_call(
        functools.partial(_pool_kernel, tr=tr),
        grid=(m // tr,),
        in_specs=[
            pl.BlockSpec((POOL_HALO, D_MODEL), lambda i: (jnp.maximum(i * hb - 1, 0), 0)),
            pl.BlockSpec((tr, D_MODEL), lambda i: (i, 0)),
            pl.BlockSpec((POOL_HALO, D_MODEL), lambda i: (jnp.minimum((i + 1) * hb, n_hb - 1), 0)),
            pl.BlockSpec((n_grp, POOL_GROUP_DIM, POOL_GROUP_DIM), lambda i: (0, 0, 0)),
            pl.BlockSpec((1, D_MODEL), lambda i: (0, 0)),
        ],
        out_specs=pl.BlockSpec((tr, D_MODEL), lambda i: (i, 0)),
        out_shape=jax.ShapeDtypeStruct((m, D_MODEL), BF16),
        scratch_shapes=[
            pltpu.VMEM((tr + 2 * POOL_HALO, D_MODEL), F32),
            pltpu.VMEM((tr, D_MODEL), BF16),
        ],
        compiler_params=_params(1),
        name="pool_mix",
    )(u, u, u, w_grp, ch_scale)


def _dil_attn_kernel(q_ref, kp_ref, kc_ref, kn_ref, vp_ref, vc_ref, vn_ref,
                     o_ref, m_ref, l_ref, *, tq, n_slots):
    i = pl.program_id(1)
    row = lax.broadcasted_iota(jnp.int32, (tq, 3 * tq), 0)
    col = lax.broadcasted_iota(jnp.int32, (tq, 3 * tq), 1)
    key_slot = i * tq - tq + col
    valid = (jnp.abs(col - tq - row) <= DIL_HALF) & (key_slot >= 0) & (key_slot < n_slots)
    for h in range(DIL_HEADS):
        hs = slice(h * DIL_HEAD_DIM, (h + 1) * DIL_HEAD_DIM)
        q = q_ref[:, hs]
        s = jnp.concatenate(
            [_bdot_t(q, kp_ref[:, hs]), _bdot_t(q, kc_ref[:, hs]), _bdot_t(q, kn_ref[:, hs])], axis=1)
        s = jnp.where(valid, s, -jnp.inf)
        m = jnp.max(s, axis=1, keepdims=True)
        p = jnp.exp(s - m)
        l = jnp.sum(p, axis=1, keepdims=True)
        pb = p.astype(BF16)
        o = (_bdot(pb[:, 0:tq], vp_ref[:, hs]) + _bdot(pb[:, tq:2 * tq], vc_ref[:, hs])
             + _bdot(pb[:, 2 * tq:], vn_ref[:, hs]))
        o_ref[:, hs] = o
        m_ref[:, hs] = jnp.broadcast_to(m, (tq, DIL_HEAD_DIM))
        l_ref[:, hs] = jnp.broadcast_to(l, (tq, DIL_HEAD_DIM))


def dil_group_attention(qkv, group, dil, *, tq):
    n_slots = SEQ // dil
    nb = n_slots // tq

    def spec(slab, shift):
        return pl.BlockSpec(
            (tq, DIL_COLS), lambda p, i: (jnp.clip(i + shift, 0, nb - 1), slab * dil + p))

    out_spec = pl.BlockSpec((tq, DIL_COLS), lambda p, i: (i, p))
    out_sds = jax.ShapeDtypeStruct((n_slots, dil * DIL_COLS), F32)
    return pl.pallas_call(
        functools.partial(_dil_attn_kernel, tq=tq, n_slots=n_slots),
        grid=(dil, nb),
        in_specs=[spec(0, 0),
                  spec(1, -1), spec(1, 0), spec(1, 1),
                  spec(2, -1), spec(2, 0), spec(2, 1)],
        out_specs=[out_spec, out_spec, out_spec],
        out_shape=[out_sds, out_sds, out_sds],
        compiler_params=_params(2),
        name=f"dil_attn_g{group}",
    )(qkv, qkv, qkv, qkv, qkv, qkv, qkv)


def _dil_merge_kernel(*refs, dils):
    n_in = 3 * len(dils)
    in_refs, out_ref, scratch = refs[:n_in], refs[n_in], list(refs[n_in + 1:])
    tm = out_ref.shape[0]
    for h in range(DIL_HEADS):
        tok = []
        for gi, dil in enumerate(dils):
            for a, src in enumerate(in_refs[3 * gi:3 * gi + 3]):
                if dil == 1:
                    tok.append(src[:, h * DIL_HEAD_DIM:(h + 1) * DIL_HEAD_DIM])
                    continue
                dst = scratch.pop(0)
                for p in range(dil):
                    lo = p * DIL_COLS + h * DIL_HEAD_DIM
                    dst[h, pl.ds(p, tm // dil, stride=dil), :] = src[:, lo:lo + DIL_HEAD_DIM]
                tok.append(dst[h])
                scratch.append(dst)
        o0, m0, l0, o1, m1, l1, o2, m2, l2 = tok
        mm = jnp.maximum(jnp.maximum(m0, m1), m2)
        w0 = jnp.exp(m0 - mm)
        w1 = jnp.exp(m1 - mm)
        w2 = jnp.exp(m2 - mm)
        num = w0 * o0 + w1 * o1 + w2 * o2
        den = w0 * l0 + w1 * l1 + w2 * l2
        out_ref[:, h * DIL_HEAD_DIM:(h + 1) * DIL_HEAD_DIM] = (num / den).astype(out_ref.dtype)


def dil_merge(parts, dils, *, tm):
    in_specs = []
    n_scratch = 0
    for dil in dils:
        in_specs += [pl.BlockSpec((tm // dil, dil * DIL_COLS), lambda i: (i, 0))] * 3
        n_scratch += 3 if dil > 1 else 0
    flat = [a for part in parts for a in part]
    return pl.pallas_call(
        functools.partial(_dil_merge_kernel, dils=tuple(dils)),
        grid=(SEQ // tm,),
        in_specs=in_specs,
        out_specs=pl.BlockSpec((tm, DIL_COLS), lambda i: (i, 0)),
        out_shape=jax.ShapeDtypeStruct((SEQ, DIL_COLS), BF16),
        scratch_shapes=[pltpu.VMEM((DIL_HEADS, tm, DIL_HEAD_DIM), F32)] * n_scratch,
        compiler_params=_params(1),
        name="dil_merge",
    )(*flat)


def _mla_prep_kernel(u_ref, qan_ref, kvan_ref, wq_ref, wkv_ref, qn_ref, kn_ref,
                     cos_ref, sa_ref, sb_ref, q_out, k_out, vt_out, cq_s, ckv_s):
    @pl.when(pl.program_id(1) == 0)
    def _():
        cq = u_ref[:, 0:MLA_Q_RANK]
        cq_s[...] = (cq * lax.rsqrt(jnp.mean(cq * cq, axis=-1, keepdims=True) + EPS)
                     * qan_ref[...]).astype(BF16)
        ckv = u_ref[:, MLA_Q_RANK:MLA_Q_RANK + MLA_KV_RANK]
        ckv_s[...] = (ckv * lax.rsqrt(jnp.mean(ckv * ckv, axis=-1, keepdims=True) + EPS)
                      * kvan_ref[...]).astype(BF16)

    cos = cos_ref[...]
    sa = sa_ref[...]
    sb = sb_ref[...]

    def norm_rope(nope, rp, w_ref):
        ssq = jnp.sum(nope * nope, axis=-1, keepdims=True) + jnp.sum(rp * rp, axis=-1, keepdims=True)
        r = lax.rsqrt(ssq * (1.0 / MLA_QK) + EPS)
        nope_n = nope * r * w_ref[:, 0:128]
        rp_n = rp * r * w_ref[:, 128:256]
        rp_r = rp_n * cos + pltpu.roll(rp_n, 96, 1) * sa + pltpu.roll(rp_n, 32, 1) * sb
        return nope_n, rp_r

    q = _bdot(cq_s[...], wq_ref[...].astype(BF16))
    kv = _bdot(ckv_s[...], wkv_ref[...].astype(BF16))
    kr = u_ref[:, MLA_Q_RANK + MLA_KV_RANK:MLA_Q_RANK + MLA_KV_RANK + 128]
    scale = MLA_QK ** -0.5 * LOG2_E
    qn, qr = norm_rope(q[:, 0:128], q[:, 128:256], qn_ref)
    q_out[:, 0:128] = (qn * scale).astype(BF16)
    q_out[:, 128:256] = (qr * scale).astype(BF16)
    kn, kr_r = norm_rope(kv[:, 0:128], kr, kn_ref)
    k_out[:, 0:128] = kn.astype(BF16)
    k_out[:, 128:256] = kr_r.astype(BF16)
    vt_out[...] = kv[:, 128:256].T.astype(BF16)


def mla_prep(u, q_a_norm, kv_a_norm, wq_pad, w_kv_up, qn_pad, kn_pad, cos, sa, sb, *, tm):
    m = u.shape[0]
    ucols = u.shape[1]
    vec = lambda n: pl.BlockSpec((1, n), lambda i, h: (0, 0))
    tab = pl.BlockSpec((tm, 128), lambda i, h: (i, 0))
    head_out = lambda w: pl.BlockSpec((None, tm, w), lambda i, h: (h, i, 0))
    return pl.pallas_call(
        _mla_prep_kernel,
        grid=(m // tm, MLA_HEADS),
        in_specs=[
            pl.BlockSpec((tm, ucols), lambda i, h: (i, 0)),
            vec(MLA_Q_RANK), vec(MLA_KV_RANK),
            pl.BlockSpec((MLA_Q_RANK, MLA_PAD), lambda i, h: (0, h)),
            pl.BlockSpec((MLA_KV_RANK, MLA_NOPE + MLA_V), lambda i, h: (0, h)),
            vec(MLA_PAD), vec(MLA_PAD),
            tab, tab, tab,
        ],
        out_specs=[head_out(MLA_PAD), head_out(MLA_PAD),
                   pl.BlockSpec((None, MLA_V, tm), lambda i, h: (h, 0, i))],
        out_shape=[
            jax.ShapeDtypeStruct((MLA_HEADS, m, MLA_PAD), BF16),
            jax.ShapeDtypeStruct((MLA_HEADS, m, MLA_PAD), BF16),
            jax.ShapeDtypeStruct((MLA_HEADS, MLA_V, m), BF16),
        ],
        scratch_shapes=[pltpu.VMEM((tm, MLA_Q_RANK), BF16), pltpu.VMEM((tm, MLA_KV_RANK), BF16)],
        compiler_params=_params(2),
        name="mla_prep",
    )(u, q_a_norm, kv_a_norm, wq_pad, w_kv_up, qn_pad, kn_pad, cos, sa, sb)


def _mla_attn_kernel(q_ref, k_ref, vt_ref, o_ref, m_s, l_s, acc_s, sa_ref, sb_ref, *, tk):
    q = q_ref[...]
    n_chunks = k_ref.shape[0] // tk
    assert n_chunks % 2 == 0
    m_s[...] = jnp.full(m_s.shape, -jnp.inf, F32)
    l_s[...] = jnp.zeros(l_s.shape, F32)
    acc_s[...] = jnp.zeros(acc_s.shape, F32)

    def key_rows(c):
        return pl.ds(pl.multiple_of(c * tk, tk), tk)

    def scores(c, dst):
        dst[...] = _bdot_t(k_ref[key_rows(c), :], q)

    def consume(c, src):
        st = src[...]
        m_prev = m_s[...]
        m_new = jnp.maximum(m_prev, jnp.max(st, axis=0, keepdims=True))
        alpha = jnp.exp2(m_prev - m_new)
        pt = jnp.exp2(st - m_new)
        l_s[...] = alpha * l_s[...] + jnp.sum(pt, axis=0, keepdims=True)
        acc_s[...] = alpha * acc_s[...] + _bdot(vt_ref[:, key_rows(c)], pt.astype(BF16))
        m_s[...] = m_new

    scores(0, sa_ref)

    def body(c2, carry):
        c = 2 * c2
        scores(c + 1, sb_ref)
        consume(c, sa_ref)
        scores(c + 2, sa_ref)
        consume(c + 1, sb_ref)
        return carry

    lax.fori_loop(0, n_chunks // 2 - 1, body, 0)
    scores(n_chunks - 1, sb_ref)
    consume(n_chunks - 2, sa_ref)
    consume(n_chunks - 1, sb_ref)
    o_ref[...] = (acc_s[...] / l_s[...]).T.astype(o_ref.dtype)


def mla_attention(q, k, vt, *, tq, tk):
    n_heads, m, _ = q.shape
    return pl.pallas_call(
        functools.partial(_mla_attn_kernel, tk=tk),
        grid=(n_heads, m // tq),
        in_specs=[
            pl.BlockSpec((None, tq, MLA_PAD), lambda h, i: (h, i, 0)),
            pl.BlockSpec((None, m, MLA_PAD), lambda h, i: (h, 0, 0)),
            pl.BlockSpec((None, MLA_V, m), lambda h, i: (h, 0, 0)),
        ],
        out_specs=pl.BlockSpec((tq, MLA_V), lambda h, i: (i, h)),
        out_shape=jax.ShapeDtypeStruct((m, n_heads * MLA_V), BF16),
        scratch_shapes=[
            pltpu.VMEM((1, tq), F32), pltpu.VMEM((1, tq), F32), pltpu.VMEM((MLA_V, tq), F32),
            pltpu.VMEM((tk, tq), F32), pltpu.VMEM((tk, tq), F32)],
        compiler_params=_params(2),
        name="mla_attention",
    )(q, k, vt)


def _router_kernel(x_ref, g_ref, sc_ref, sh_ref, wr_ref, hn_out, aff_out):
    hn = _norm_mod(x_ref[...], g_ref[...], sc_ref[...], sh_ref[...])
    hn_out[...] = hn.astype(BF16)
    logits = jnp.dot(hn, wr_ref[...], preferred_element_type=F32, precision=lax.Precision.HIGHEST)
    z = jnp.exp(logits - jnp.max(logits, axis=-1, keepdims=True))
    aff_out[...] = z / jnp.sum(z, axis=-1, keepdims=True)


def moe_router(x, g, scale, shift, w_router, *, tm):
    m = x.shape[0]
    vec = pl.BlockSpec((1, D_MODEL), lambda i: (0, 0))
    return pl.pallas_call(
        _router_kernel,
        grid=(m // tm,),
        in_specs=[pl.BlockSpec((tm, D_MODEL), lambda i: (i, 0)), vec, vec, vec,
                  pl.BlockSpec((D_MODEL, N_EXPERTS), lambda i: (0, 0))],
        out_specs=[pl.BlockSpec((tm, D_MODEL), lambda i: (i, 0)),
                   pl.BlockSpec((tm, N_EXPERTS), lambda i: (i, 0))],
        out_shape=[jax.ShapeDtypeStruct((m, D_MODEL), BF16),
                   jax.ShapeDtypeStruct((m, N_EXPERTS), F32)],
        compiler_params=_params(1),
        name="moe_router",
    )(x, g, scale, shift, w_router)


def _ffn_kernel(xs_ref, wg_ref, wu_ref, wd_ref, gt_ref, rg_ref, o_ref):
    f = pl.program_id(1)
    xs = xs_ref[...]
    a = _bdot(xs, wg_ref[...].astype(BF16))
    b = _bdot(xs, wu_ref[...].astype(BF16))
    hmid = (a * jax.nn.sigmoid(a) * b).astype(BF16)
    y = _bdot(hmid, wd_ref[...].astype(BF16))
    last = f == pl.num_programs(1) - 1

    @pl.when(f == 0)
    def _():
        o_ref[...] = y

    @pl.when(jnp.logical_and(f > 0, jnp.logical_not(last)))
    def _():
        o_ref[...] += y

    @pl.when(last)
    def _():
        o_ref[...] = (o_ref[...] + y) * gt_ref[...] * (1.0 + rg_ref[...])


def expert_ffn(xs, w_gate, w_up, w_down, gates, res_gate, *, layer, tf):
    n_exp, cap, _ = xs.shape
    return pl.pallas_call(
        _ffn_kernel,
        grid=(n_exp, EXPERT_FF // tf),
        in_specs=[
            pl.BlockSpec((None, cap, D_MODEL), lambda e, f: (e, 0, 0)),
            pl.BlockSpec((None, None, D_MODEL, tf), lambda e, f: (layer, e, 0, f)),
            pl.BlockSpec((None, None, D_MODEL, tf), lambda e, f: (layer, e, 0, f)),
            pl.BlockSpec((None, None, tf, D_MODEL), lambda e, f: (layer, e, f, 0)),
            pl.BlockSpec((None, cap, 1), lambda e, f: (e, 0, 0)),
            pl.BlockSpec((1, D_MODEL), lambda e, f: (0, 0)),
        ],
        out_specs=pl.BlockSpec((None, cap, D_MODEL), lambda e, f: (e, 0, 0)),
        out_shape=jax.ShapeDtypeStruct((n_exp, cap, D_MODEL), F32),
        compiler_params=_params(2),
        name="expert_ffn",
    )(xs, w_gate, w_up, w_down, gates, res_gate)


def moe_layer(x, g, scale, shift, gate, w_router, w_gate, w_up, w_down, *, layer):
    hn, aff = moe_router(x, g, scale, shift, w_router, tm=512)
    gates, idx = lax.top_k(aff.T, EXPERT_CAP)
    xs = jnp.take(hn, idx.reshape(-1), axis=0).reshape(N_EXPERTS, EXPERT_CAP, D_MODEL)
    y = expert_ffn(xs, w_gate, w_up, w_down, gates[..., None], gate, layer=layer, tf=256)
    return x.at[idx.reshape(-1)].add(y.reshape(-1, D_MODEL))


def _rope_tables(positions, dim):
    half = dim // 2
    inv = ROPE_THETA ** (-jnp.arange(half, dtype=F32) / half)
    ang = positions.astype(F32)[:, None] * inv
    return jnp.cos(ang), jnp.sin(ang)


def _row(v):
    return v.reshape(1, -1)


def conv_mixer(xs, g, shift, scale, gate, w_in, dw, ln_g, ln_b, w_out):
    u = norm_mod_matmul_glu(xs, g, scale, shift, w_in, tm=1024, tn=512, out_dtype=F32)
    v = conv_ln_swish(u, dw, _row(ln_g), _row(ln_b), tr=256)
    return matmul_residual(v, w_out, xs, gate, tm=2048, tn=512)


def pool_mixer(xs, g, shift, scale, gate, w_in, w_grp, ch_scale, w_out):
    u = norm_mod_matmul(xs, g, scale, shift, w_in, tm=1024, tn=512, out_dtype=F32)
    v = pool_mix(u, w_grp, _row(ch_scale), tr=256)
    return matmul_residual(v, w_out, xs, gate, tm=2048, tn=512)


def dil_mixer(xs, g, shift, scale, gate, pos, w_in, q_norm, k_norm, w_out):
    cos, sin = _rope_tables(pos, DIL_HEAD_DIM)
    cos_f = jnp.concatenate([cos, cos], axis=1)
    sin_f = jnp.concatenate([-sin, sin], axis=1)
    ones = jnp.ones((DIL_HEAD_DIM,), F32)
    nw = jnp.stack([w for grp in range(len(DIL_PATTERNS))
                    for w in (q_norm[grp], k_norm[grp], ones)])[:, None, :]
    qscale = jnp.asarray([DIL_HEAD_DIM ** -0.5, 1.0, 1.0] * len(DIL_PATTERNS), F32)
    qs = jnp.broadcast_to(qscale[:, None, None], nw.shape)
    dils = [dil for _, dil in DIL_PATTERNS]
    parts = []
    for grp, dil in enumerate(dils):
        qkv = norm_mod_matmul_dil(xs, g, scale, shift, w_in, nw, qs, cos_f, sin_f,
                                  group=grp, dil=dil, tm=1024)
        parts.append(dil_group_attention(qkv, grp, dil, tq=256))
    o = dil_merge(parts, dils, tm=256)
    return matmul_residual(o, w_out, xs, gate, tm=2048, tn=512)


def mla_mixer(xs, g, shift, scale, gate, pos, w_in, q_a_norm, w_q_up, kv_a_norm, w_kv_up,
              q_norm, k_norm, w_out):
    cos, sin = _rope_tables(pos, MLA_ROPE)
    z32 = jnp.zeros_like(sin)
    z64 = jnp.zeros((SEQ, 64), F32)
    cos_p = jnp.concatenate([cos, cos, z64], axis=1)
    sin_a = jnp.concatenate([-sin, z32, z64], axis=1)
    sin_b = jnp.concatenate([z32, sin, z64], axis=1)
    w_in_pad = jnp.pad(w_in, ((0, 0), (0, 64)))
    wq_pad = jnp.pad(w_q_up.reshape(MLA_Q_RANK, MLA_HEADS, MLA_QK),
                     ((0, 0), (0, 0), (0, MLA_PAD - MLA_QK))).reshape(MLA_Q_RANK, MLA_HEADS * MLA_PAD)
    qn_pad = jnp.pad(q_norm, (0, MLA_PAD - MLA_QK)).reshape(1, MLA_PAD)
    kn_pad = jnp.pad(k_norm, (0, MLA_PAD - MLA_QK)).reshape(1, MLA_PAD)
    u = norm_mod_matmul(xs, g, scale, shift, w_in_pad, tm=512, tn=w_in_pad.shape[1], out_dtype=F32)
    q, k, vt = mla_prep(u, _row(q_a_norm), _row(kv_a_norm), wq_pad, w_kv_up, qn_pad, kn_pad,
                       cos_p, sin_a, sin_b, tm=512)
    o = mla_attention(q, k, vt, tq=1024, tk=2048)
    return matmul_residual(o, w_out, xs, gate, tm=2048, tn=512)


def kernel(x, c, positions, norm_g, ada_w, ada_b, conv_w_in, conv_dw, conv_ln_g, conv_ln_b, conv_w_out, pool_w_in, pool_w_grp, pool_scale, pool_w_out, dil_w_in, dil_q_norm, dil_k_norm, dil_w_out, mla_w_in, mla_q_a_norm, mla_w_q_up, mla_kv_a_norm, mla_w_kv_up, mla_q_norm, mla_k_norm, mla_w_out, moe_router, moe_w_gate, moe_w_up, moe_w_down):
    assert x.shape == (1, SEQ, D_MODEL) and ada_w.shape[0] == DEPTH
    xs = x.reshape(SEQ, D_MODEL)
    pos = positions.reshape(SEQ)
    mod = ada_all(c, ada_w, ada_b)

    def modulation(layer, sub):
        mrow = mod[2 * layer + sub]
        return mrow[:, 0:D_MODEL], mrow[:, D_MODEL:2 * D_MODEL], mrow[:, 2 * D_MODEL:]

    for layer in range(DEPTH):
        mixer = layer % 4
        occ = layer // 4
        shift, scale, gate = modulation(layer, 0)
        g = _row(norm_g[layer, 0])
        if mixer == 0:
            xs = conv_mixer(xs, g, shift, scale, gate, conv_w_in[occ], conv_dw[occ], conv_ln_g[occ],
                            conv_ln_b[occ], conv_w_out[occ])
        elif mixer == 1:
            xs = pool_mixer(xs, g, shift, scale, gate, pool_w_in[occ], pool_w_grp[occ], pool_scale[occ],
                            pool_w_out[occ])
        elif mixer == 2:
            xs = dil_mixer(xs, g, shift, scale, gate, pos, dil_w_in[occ], dil_q_norm[occ], dil_k_norm[occ],
                           dil_w_out[occ])
        else:
            xs = mla_mixer(xs, g, shift, scale, gate, pos, mla_w_in[occ], mla_q_a_norm[occ], mla_w_q_up[occ],
                           mla_kv_a_norm[occ], mla_w_kv_up[occ], mla_q_norm[occ], mla_k_norm[occ],
                           mla_w_out[occ])
        shift, scale, gate = modulation(layer, 1)
        xs = moe_layer(xs, _row(norm_g[layer, 1]), scale, shift, gate, moe_router[layer],
                       moe_w_gate, moe_w_up, moe_w_down, layer=layer)
    return xs.reshape(1, SEQ, D_MODEL)
```
